```python
import math
import jax, jax.numpy as jnp
from jax import lax
import numpy as np

D_MODEL = 1024
BATCH = 2
SEQ = 8192
DEPTH = 4
DEC_BATCH = 32
DEC_SEQ = 1
PAST_LEN = 8192
PAGE_SIZE = 128

HEAD_DIM = 64
GDN_HEADS = D_MODEL // 256
GDN_WIDTH = GDN_HEADS * HEAD_DIM
CONV_W = 4
GDN_CHUNK = 64
NSA_HEADS = D_MODEL // 128
NSA_KV = NSA_HEADS // 4
NSA_REP = NSA_HEADS // NSA_KV
NSA_WIDTH = NSA_HEADS * HEAD_DIM
NSA_KV_WIDTH = NSA_KV * HEAD_DIM
L_CMP = 32
STRIDE_CMP = 16
L_SEL = 64
TOP_SEL = 16
WINDOW = 512
NSA_QBLK = 128
ROT_DIM = HEAD_DIM // 4
ROPE_THETA = 500000.0
RWKV_HEADS = D_MODEL // 256
RWKV_WIDTH = RWKV_HEADS * HEAD_DIM
LORA_W = 64
LORA_A = 64
LORA_G = 128
GDN_COLS = 4 * GDN_WIDTH + 2 * GDN_HEADS
NSA_COLS = NSA_WIDTH + 6 * NSA_KV_WIDTH + 3 * NSA_HEADS
RWKV_COLS = 3 * RWKV_WIDTH + LORA_W + LORA_A + LORA_G
D_IN = GDN_COLS + NSA_COLS + RWKV_COLS
D_MIX = GDN_WIDTH + NSA_WIDTH + RWKV_WIDTH
N_KEYS = 128
N_EXPERTS = N_KEYS * N_KEYS
PEER_HEADS = 8
PEER_TOPK = 16
D_KEY = 256
PEER_BLK = 128
RMS_EPS = 1e-6
GN_EPS = 64e-5
NEG_INF = -1e30
FORCE = 1e4

kernel_name = 'hybrid_gdn_nsa_rwkv7_peer_decoder_step'


def rmsnorm(x, g):
    xf = x.astype(jnp.float32)
    y = xf * lax.rsqrt(jnp.mean(xf * xf, axis=-1, keepdims=True) + RMS_EPS)
    return (y * g.astype(jnp.float32)).astype(x.dtype)


def l2norm(x):
    xf = x.astype(jnp.float32)
    return (xf * lax.rsqrt(jnp.sum(xf * xf, axis=-1, keepdims=True) + RMS_EPS)).astype(x.dtype)


def masked_softmax(s, mask):
    p = jax.nn.softmax(jnp.where(mask, s, NEG_INF), axis=-1)
    return p * mask


def rope_partial(x, pos):
    half = ROT_DIM // 2
    inv = ROPE_THETA ** (-jnp.arange(half, dtype=jnp.float32) / half)
    ang = pos.astype(jnp.float32)[:, None] * inv[None, :]
    shp = (pos.shape[0],) + (1,) * (x.ndim - 3) + (half,)
    cos, sin = jnp.cos(ang).reshape(shp), jnp.sin(ang).reshape(shp)
    xr = x[..., :ROT_DIM].astype(jnp.float32)
    x1, x2 = xr[..., :half], xr[..., half:]
    rot = jnp.concatenate([x1 * cos - x2 * sin, x2 * cos + x1 * sin], axis=-1).astype(x.dtype)
    return jnp.concatenate([rot, x[..., ROT_DIM:]], axis=-1)


def short_conv(x, buf, w):
    T = x.shape[1]
    xp = jnp.concatenate([buf, x], axis=1)
    y = w[0] * xp[:, 0:T]
    for j in range(1, CONV_W):
        y = y + w[j] * xp[:, j:j + T]
    return jax.nn.silu(y), xp[:, T:]


def gdn_inputs(cols, conv0, conv_w, A_log, dt_bias):
    B, T, _ = cols.shape
    W = GDN_WIDTH
    y, conv_new = short_conv(cols[..., :3 * W], conv0, conv_w)
    y = y.reshape(B, T, 3, GDN_HEADS, HEAD_DIM)
    q = l2norm(y[:, :, 0]) * (HEAD_DIM ** -0.5)
    k = l2norm(y[:, :, 1])
    v = y[:, :, 2]
    a_logit = cols[..., 3 * W:3 * W + GDN_HEADS].astype(jnp.float32)
    b_logit = cols[..., 3 * W + GDN_HEADS:3 * W + 2 * GDN_HEADS].astype(jnp.float32)
    g = -jnp.exp(A_log.astype(jnp.float32)) * jax.nn.softplus(a_logit + dt_bias.astype(jnp.float32))
    beta = jax.nn.sigmoid(b_logit)
    gate = cols[..., 3 * W + 2 * GDN_HEADS:]
    return q, k, v, g, beta, gate, conv_new


def gdn_chunked(q, k, v, g, beta, S0):
    B, T, H, D = q.shape
    C = GDN_CHUNK
    N = T // C
    f32 = jnp.float32
    ch = lambda t: t.astype(f32).reshape(B, N, C, H, D).transpose(0, 1, 3, 2, 4)
    qc, kc, vc = ch(q), ch(k), ch(v)
    G = jnp.cumsum(g.reshape(B, N, C, H).transpose(0, 1, 3, 2), axis=-1)
    bc = beta.reshape(B, N, C, H).transpose(0, 1, 3, 2)
    idx = jnp.arange(C)
    lower_incl = idx[:, None] >= idx[None, :]
    lower_strict = idx[:, None] > idx[None, :]
    diff = G[..., :, None] - G[..., None, :]
    decay_incl = jnp.where(lower_incl, jnp.exp(jnp.where(lower_incl, diff, 0.0)), 0.0)
    kb = kc * bc[..., None]
    L = jnp.einsum('bnhid,bnhjd->bnhij', kb, kc) * jnp.where(lower_strict, decay_incl, 0.0)
    A = jnp.eye(C, dtype=f32) + L
    rhs = jnp.concatenate([vc * bc[..., None], kb * jnp.exp(G)[..., None]], axis=-1)
    sol = lax.linalg.triangular_solve(A, rhs, left_side=True, lower=True, unit_diagonal=True)
    U, Wk = sol[..., :D], sol[..., D:]
    qg = qc * jnp.exp(G)[..., None]
    kdec = kc * jnp.exp(G[..., -1:] - G)[..., None]
    att = jnp.einsum('bnhid,bnhjd->bnhij', qc, kc) * decay_incl
    glast = jnp.exp(G[..., -1])

    def step(S, xs):
        U_i, W_i, qg_i, kd_i, at_i, gl_i = xs
        v_new = U_i - jnp.einsum('bhcd,bhde->bhce', W_i, S)
        o = jnp.einsum('bhcd,bhde->bhce', qg_i, S) + jnp.einsum('bhij,bhje->bhie', at_i, v_new)
        S = S * gl_i[..., None, None] + jnp.einsum('bhcd,bhce->bhde', kd_i, v_new)
        return S, o

    xs = tuple(jnp.moveaxis(t, 1, 0) for t in (U, Wk, qg, kdec, att, glast))
    S, o = lax.scan(step, S0.astype(f32), xs)
    o = jnp.moveaxis(o, 0, 1).transpose(0, 1, 3, 2, 4).reshape(B, T, H, D)
    return o, S


def gdn_recurrent(q, k, v, g, beta, S0):
    def step(S, xs):
        q_t, k_t, v_t, g_t, b_t = xs
        S = S * jnp.exp(g_t)[..., None, None]
        v_old = jnp.einsum('bhd,bhde->bhe', k_t, S)
        S = S + jnp.einsum('bhd,bhe->bhde', k_t, b_t[..., None] * (v_t - v_old))
        return S, jnp.einsum('bhd,bhde->bhe', q_t, S)

    xs = tuple(jnp.moveaxis(t.astype(jnp.float32), 1, 0) for t in (q, k, v, g, beta))
    S, o = lax.scan(step, S0.astype(jnp.float32), xs)
    return jnp.moveaxis(o, 0, 1), S


def nsa_project(cols, pos):
    B, T, _ = cols.shape
    q = rope_partial(cols[..., :NSA_WIDTH].reshape(B, T, NSA_HEADS, HEAD_DIM), pos)
    q = q.reshape(B, T, NSA_KV, NSA_REP, HEAD_DIM)
    kv = cols[..., NSA_WIDTH:NSA_WIDTH + 6 * NSA_KV_WIDTH].reshape(B, T, 3, 2, NSA_KV, HEAD_DIM)
    k_rot = rope_partial(kv[:, :, :, 0], pos)
    kv = jnp.stack([k_rot, kv[:, :, :, 1]], axis=3).transpose(2, 0, 3, 1, 4, 5)
    gates = jax.nn.sigmoid(cols[..., NSA_WIDTH + 6 * NSA_KV_WIDTH:]).reshape(B, T, NSA_KV, NSA_REP, 3)
    return q, gates, kv[0], kv[1], kv[2]


def nsa_compress(rows, pe, w1, w2):
    B, T = rows.shape[0], rows.shape[1]
    nc = (T - L_CMP) // STRIDE_CMP + 1
    idx = np.arange(nc)[:, None] * STRIDE_CMP + np.arange(L_CMP)[None, :]
    blk = rows[:, idx] + pe[None, None, :, None, :]
    blk = blk.transpose(0, 1, 3, 2, 4).reshape(B, nc, NSA_KV, L_CMP * HEAD_DIM)
    return jax.nn.gelu(blk @ w1, approximate=False) @ w2


def selection_map(n_cmp, n_sel):
    i = np.arange(n_cmp)[:, None, None, None]
    j = np.arange(n_sel)[None, :, None, None]
    m = np.arange(L_SEL // STRIDE_CMP)[None, None, :, None]
    n = np.arange(L_CMP // STRIDE_CMP)[None, None, None, :]
    cnt = np.sum(i == (L_SEL // STRIDE_CMP) * j + m - n, axis=(2, 3))
    return jnp.asarray(cnt.astype(np.float32))


def nsa_attend(q, qpos, kc, vc, ks, vs, kw, vw, wpos, gates):
    B, Q = q.shape[0], q.shape[1]
    T, nc = ks.shape[1], kc.shape[1]
    n_sel = -(-T // L_SEL)
    top = min(TOP_SEL, n_sel)
    scale = HEAD_DIM ** -0.5
    cend = jnp.arange(nc, dtype=jnp.int32) * STRIDE_CMP + (L_CMP - 1)
    mc = (cend[None, :] <= qpos[:, None])[None, :, None, None, :]
    pc = masked_softmax(jnp.einsum('bqgrd,bngd->bqgrn', q, kc).astype(jnp.float32) * scale, mc)
    o_c = jnp.einsum('bqgrn,bngd->bqgrd', pc.astype(vc.dtype), vc)
    imp = jnp.einsum('bqgrn,ns->bqgs', pc, selection_map(nc, n_sel))
    blk = jnp.arange(n_sel, dtype=jnp.int32)[None, :]
    cur = (qpos // L_SEL)[:, None]
    forced = (blk == 0) | (blk == cur) | (blk == cur - 1)
    imp = jnp.where((blk > cur)[None, :, None, :], -FORCE, jnp.where(forced[None, :, None, :], FORCE, imp))
    _, sel = lax.top_k(imp, top)
    tok = (sel[..., None] * L_SEL + jnp.arange(L_SEL, dtype=jnp.int32)).reshape(B, Q, NSA_KV, top * L_SEL)
    ms = (tok <= qpos[None, :, None, None])[:, :, :, None, :]
    tok = jnp.minimum(tok, T - 1)
    bi = jnp.arange(B)[:, None, None, None]
    gi = jnp.arange(NSA_KV)[None, None, :, None]
    ks_g, vs_g = ks[bi, tok, gi], vs[bi, tok, gi]
    ps = masked_softmax(jnp.einsum('bqgrd,bqgnd->bqgrn', q, ks_g).astype(jnp.float32) * scale, ms)
    o_s = jnp.einsum('bqgrn,bqgnd->bqgrd', ps.astype(vs.dtype), vs_g)
    mw = ((wpos[None, :] <= qpos[:, None]) & (wpos[None, :] > qpos[:, None] - WINDOW)
          & (wpos[None, :] >= 0))[None, :, None, None, :]
    pw = masked_softmax(jnp.einsum('bqgrd,bwgd->bqgrw', q, kw).astype(jnp.float32) * scale, mw)
    o_w = jnp.einsum('bqgrw,bwgd->bqgrd', pw.astype(vw.dtype), vw)
    return gates[..., 0:1] * o_c + gates[..., 1:2] * o_s + gates[..., 2:3] * o_w


def nsa_prompt(q, gates, kv_c, kv_s, kv_w, cmp_w):
    pe, w1k, w2k, w1v, w2v = cmp_w
    B, T = q.shape[0], q.shape[1]
    kc = nsa_compress(kv_c[:, 0], pe, w1k, w2k)
    vc = nsa_compress(kv_c[:, 1], pe, w1v, w2v)
    ks, vs = kv_s[:, 0], kv_s[:, 1]
    kw_pad = jnp.pad(kv_w, ((0, 0), (0, 0), (WINDOW, 0), (0, 0), (0, 0)))

    def block(i):
        s0 = i * NSA_QBLK
        qb = lax.dynamic_slice_in_dim(q, s0, NSA_QBLK, axis=1)
        gb = lax.dynamic_slice_in_dim(gates, s0, NSA_QBLK, axis=1)
        wb = lax.dynamic_slice_in_dim(kw_pad, s0, NSA_QBLK + WINDOW, axis=2)
        qpos = s0 + jnp.arange(NSA_QBLK, dtype=jnp.int32)
        wpos = s0 - WINDOW + jnp.arange(NSA_QBLK + WINDOW, dtype=jnp.int32)
        return nsa_attend(qb, qpos, kc, vc, ks, vs, wb[:, 0], wb[:, 1], wpos, gb)

    o = lax.map(block, jnp.arange(T // NSA_QBLK, dtype=jnp.int32))
    return jnp.moveaxis(o, 0, 1).reshape(B, T, NSA_WIDTH)


def gather_pages(cache, page_table):
    g = cache[page_table]
    DB, NP = page_table.shape
    return g.transpose(0, 2, 1, 3, 4, 5).reshape(DB, 2, NP * PAGE_SIZE, NSA_KV, HEAD_DIM)


def nsa_sample(q, gates, kv_c, kv_s, kv_w, win_buf, pg, cmp_w):
    cache_c, cache_s, page_table = pg
    pe, w1k, w2k, w1v, w2v = cmp_w
    B, t_new = q.shape[0], q.shape[1]
    full_c = jnp.concatenate([gather_pages(cache_c, page_table), kv_c], axis=2)
    full_s = jnp.concatenate([gather_pages(cache_s, page_table), kv_s], axis=2)
    kc = nsa_compress(full_c[:, 0], pe, w1k, w2k)
    vc = nsa_compress(full_c[:, 1], pe, w1v, w2v)
    kw = jnp.concatenate([win_buf, kv_w], axis=2)
    wb = win_buf.shape[2]
    past = full_s.shape[2] - t_new
    qpos = past + jnp.arange(t_new, dtype=jnp.int32)
    wpos = past - wb + jnp.arange(wb + t_new, dtype=jnp.int32)
    o = nsa_attend(q, qpos, kc, vc, full_s[:, 0], full_s[:, 1], kw[:, 0], kw[:, 1], wpos, gates)
    return o.reshape(B, t_new, NSA_WIDTH), kw[:, :, t_new:]


def rwkv_inputs(cols, shift0, mu, w0, w2, a0, a2, g2, k_k, k_a):
    B, T, _ = cols.shape
    prev = jnp.concatenate([shift0, cols[:, :-1]], axis=1)
    xs = cols + (prev - cols) * mu
    W = RWKV_WIDTH
    r, k, v = xs[..., :W], xs[..., W:2 * W], xs[..., 2 * W:3 * W]
    xw = xs[..., 3 * W:3 * W + LORA_W]
    xa = xs[..., 3 * W + LORA_W:3 * W + LORA_W + LORA_A]
    xg = xs[..., 3 * W + LORA_W + LORA_A:]
    wlog = -jax.nn.softplus(-(w0 + jnp.tanh(xw) @ w2)) - 0.5
    decay = jnp.exp(-jnp.exp(wlog.astype(jnp.float32)))
    a = jax.nn.sigmoid(a0 + xa @ a2)
    gout = jax.nn.sigmoid(xg) @ g2
    heads = lambda t: t.reshape(B, T, RWKV_HEADS, HEAD_DIM)
    kk = l2norm(heads(k * k_k))
    k = k * (1 + (a - 1) * k_a)
    return heads(r), heads(decay), heads(k), heads(v), kk, heads(a), gout, cols[:, -1:]


def rwkv_scan(r, decay, k, v, kk, a, S0):
    def step(S, xs):
        r_t, w_t, k_t, v_t, kk_t, a_t = xs
        sa = jnp.einsum('bhij,bhj->bhi', S, -kk_t)
        S = S * w_t[:, :, None, :] + sa[..., :, None] * (kk_t * a_t)[..., None, :] + v_t[..., :, None] * k_t[..., None, :]
        return S, jnp.einsum('bhij,bhj->bhi', S, r_t)

    xs = tuple(jnp.moveaxis(t.astype(jnp.float32), 1, 0) for t in (r, decay, k, v, kk, a))
    S, o = lax.scan(step, S0.astype(jnp.float32), xs)
    return jnp.moveaxis(o, 0, 1), S


def rwkv_out(o, r, k, v, gout, r_k, ln_g, ln_b):
    B, T = o.shape[0], o.shape[1]
    of = o.astype(jnp.float32)
    mean = jnp.mean(of, axis=-1, keepdims=True)
    var = jnp.mean(jnp.square(of - mean), axis=-1, keepdims=True)
    on = ((of - mean) * lax.rsqrt(var + GN_EPS)).astype(o.dtype).reshape(B, T, RWKV_WIDTH) * ln_g + ln_b
    bonus = (jnp.sum(r * k * r_k, axis=-1, keepdims=True) * v).reshape(B, T, RWKV_WIDTH)
    return (on + bonus) * gout


def peer_ffn(h, w_q, k1, k2, u_tab, v_tab):
    B, T, D = h.shape
    n = B * T
    blk = min(PEER_BLK, n)
    n_pad = -(-n // blk) * blk
    xf = jnp.pad(h.reshape(n, D), ((0, n_pad - n), (0, 0)))

    def one(xb):
        q = (xb @ w_q).reshape(-1, PEER_HEADS, 2, D_KEY // 2)
        s1 = jnp.einsum('thd,hnd->thn', q[:, :, 0], k1).astype(jnp.float32)
        s2 = jnp.einsum('thd,hnd->thn', q[:, :, 1], k2).astype(jnp.float32)
        v1, i1 = lax.top_k(s1, PEER_TOPK)
        v2, i2 = lax.top_k(s2, PEER_TOPK)
        cand = (v1[..., :, None] + v2[..., None, :]).reshape(-1, PEER_HEADS, PEER_TOPK * PEER_TOPK)
        cid = (i1[..., :, None] * N_KEYS + i2[..., None, :]).reshape(-1, PEER_HEADS, PEER_TOPK * PEER_TOPK)
        sc, pos = lax.top_k(cand, PEER_TOPK)
        eid = jnp.take_along_axis(cid, pos, axis=-1)
        gw = jax.nn.softmax(sc, axis=-1).astype(xb.dtype)
        act = jax.nn.gelu(jnp.einsum('td,thkd->thk', xb, u_tab[eid]), approximate=False) * gw
        return jnp.einsum('thk,thkd->td', act, v_tab[eid])

    out = lax.map(one, xf.reshape(-1, blk, D)).reshape(n_pad, D)[:n]
    return out.reshape(B, T, D)


def mixer_block(h, pos, p, l, st, pg):
    B, T, _ = h.shape
    gdn_s0, conv0, rwkv_s0, shift0, win0 = st
    cols = h @ p['w_in'][l]
    c_gdn = cols[..., :GDN_COLS]
    c_nsa = cols[..., GDN_COLS:GDN_COLS + NSA_COLS]
    c_rwkv = cols[..., GDN_COLS + NSA_COLS:]
    q, k, v, g, beta, gate, conv_new = gdn_inputs(c_gdn, conv0, p['gdn_conv_w'][l], p['gdn_A_log'][l], p['gdn_dt_bias'][l])
    if pg is None:
        o_a, s_a = gdn_chunked(q, k, v, g, beta, gdn_s0)
    else:
        o_a, s_a = gdn_recurrent(q, k, v, g, beta, gdn_s0)
    y_a = rmsnorm(o_a.astype(h.dtype), p['gdn_norm_g'][l]).reshape(B, T, GDN_WIDTH) * jax.nn.silu(gate)
    qn, gates, kv_c, kv_s, kv_w = nsa_project(c_nsa, pos)
    cmp_w = (p['nsa_cmp_pos'][l], p['nsa_cmp_w1k'][l], p['nsa_cmp_w2k'][l], p['nsa_cmp_w1v'][l], p['nsa_cmp_w2v'][l])
    if pg is None:
        y_b = nsa_prompt(qn, gates, kv_c, kv_s, kv_w, cmp_w)
        win_new = kv_w[:, :, T - min(WINDOW, T):]
    else:
        y_b, win_new = nsa_sample(qn, gates, kv_c, kv_s, kv_w, win0, pg, cmp_w)
    r, w, kr, vr, kk, a, gout, shift_new = rwkv_inputs(
        c_rwkv, shift0, p['rwkv_mu'][l], p['rwkv_w0'][l], p['rwkv_w2'][l], p['rwkv_a0'][l], p['rwkv_a2'][l],
        p['rwkv_g2'][l], p['rwkv_k_k'][l], p['rwkv_k_a'][l])
    o_c, s_c = rwkv_scan(r, w, kr, vr, kk, a, rwkv_s0)
    y_c = rwkv_out(o_c.astype(h.dtype), r, kr, vr, gout, p['rwkv_r_k'][l], p['rwkv_ln_g'][l], p['rwkv_ln_b'][l])
    y = jnp.concatenate([y_a, y_b, y_c], axis=-1) @ p['w_out'][l]
    return y, (kv_c, kv_s, win_new, s_a.astype(h.dtype), conv_new, s_c.astype(h.dtype), shift_new)


def trunk(x, c, pos, p, init, paged):
    B = x.shape[0]
    dt = x.dtype
    outs = ([], [], [], [], [], [], [])
    for l in range(DEPTH):
        ada = (jax.nn.silu(c) @ p['w_ada'][l] + p['b_ada'][l])[:, None, :]
        sh1, sc1, g1, sh2, sc2, g2 = jnp.split(ada, 6, axis=-1)
        if init is None:
            st = (jnp.zeros((B, GDN_HEADS, HEAD_DIM, HEAD_DIM), dt), jnp.zeros((B, CONV_W - 1, 3 * GDN_WIDTH), dt),
                  jnp.zeros((B, RWKV_HEADS, HEAD_DIM, HEAD_DIM), dt), jnp.zeros((B, 1, RWKV_COLS), dt), None)
            pg = None
        else:
            st = (init[0][l], init[1][l], init[2][l], init[3][l], init[4][l])
            pg = (paged[0][l], paged[1][l], paged[2])
        h = rmsnorm(x, p['norm_mix_g'][l]) * (1 + sc1) + sh1
        y, new = mixer_block(h, pos, p, l, st, pg)
        x = x + g1 * y
        h = rmsnorm(x, p['norm_ffn_g'][l]) * (1 + sc2) + sh2
        x = x + g2 * peer_ffn(h, p['peer_w_q'][l], p['peer_k1'][l], p['peer_k2'][l], p['peer_u'][l], p['peer_v'][l])
        for o, n in zip(outs, new):
            o.append(n)
    return rmsnorm(x, p['norm_final_g']), [jnp.stack(o) for o in outs]


def setup_inputs(seed: int = 0) -> dict:
    key = jax.random.key(seed)
    ks = list(jax.random.split(key, 64))
    f32 = jnp.float32

    def nrm(shape, s):
        return jax.random.normal(ks.pop(), shape, f32) * s

    def gain(shape):
        return 1.0 + nrm(shape, 0.02)

    n_pages = PAST_LEN // PAGE_SIZE
    n_used = DEC_BATCH * n_pages
    n_pool = n_used + n_used // 4
    w_buf = min(WINDOW, PAST_LEN)
    inp = {}
    inp['x_prompt'] = nrm((BATCH, SEQ, D_MODEL), 1.0)
    inp['x_sample'] = nrm((DEC_BATCH, DEC_SEQ, D_MODEL), 1.0)
    inp['cache_nsa_cmp'] = nrm((DEPTH, n_pool, 2, PAGE_SIZE, NSA_KV, HEAD_DIM), 1.0)
    inp['cache_nsa_sel'] = nrm((DEPTH, n_pool, 2, PAGE_SIZE, NSA_KV, HEAD_DIM), 1.0)
    inp['state_nsa_win'] = nrm((DEPTH, DEC_BATCH, 2, w_buf, NSA_KV, HEAD_DIM), 1.0)
    inp['state_gdn'] = nrm((DEPTH, DEC_BATCH, GDN_HEADS, HEAD_DIM, HEAD_DIM), 0.1)
    inp['state_gdn_conv'] = nrm((DEPTH, DEC_BATCH, CONV_W - 1, 3 * GDN_WIDTH), 1.0)
    inp['state_rwkv'] = nrm((DEPTH, DEC_BATCH, RWKV_HEADS, HEAD_DIM, HEAD_DIM), 0.1)
    inp['state_rwkv_shift'] = nrm((DEPTH, DEC_BATCH, 1, RWKV_COLS), 1.0)
    inp['page_table'] = jax.random.permutation(ks.pop(), n_pool)[:n_used].reshape(DEC_BATCH, n_pages).astype(jnp.int32)
    inp['c_prompt'] = nrm((BATCH, D_MODEL), 1.0)
    inp['c_sample'] = nrm((DEC_BATCH, D_MODEL), 1.0)
    inp['norm_mix_g'] = gain((DEPTH, D_MODEL))
    inp['norm_ffn_g'] = gain((DEPTH, D_MODEL))
    inp['norm_final_g'] = gain((D_MODEL,))
    inp['w_ada'] = nrm((DEPTH, D_MODEL, 6 * D_MODEL), 0.5 * D_MODEL ** -0.5)
    inp['b_ada'] = nrm((DEPTH, 6 * D_MODEL), 0.02)
    inp['w_in'] = nrm((DEPTH, D_MODEL, D_IN), D_MODEL ** -0.5)
    inp['w_out'] = nrm((DEPTH, D_MIX, D_MODEL), D_MIX ** -0.5)
    inp['gdn_conv_w'] = nrm((DEPTH, CONV_W, 3 * GDN_WIDTH), 0.5)
    inp['gdn_A_log'] = jnp.log(jax.random.uniform(ks.pop(), (DEPTH, GDN_HEADS), f32, 1.0, 16.0))
    dt = jnp.exp(jax.random.uniform(ks.pop(), (DEPTH, GDN_HEADS), f32, math.log(1e-3), math.log(1e-1)))
    inp['gdn_dt_bias'] = dt + jnp.log(-jnp.expm1(-dt))
    inp['gdn_norm_g'] = gain((DEPTH, HEAD_DIM))
    inp['nsa_cmp_pos'] = nrm((DEPTH, L_CMP, HEAD_DIM), 0.02)
    inp['nsa_cmp_w1k'] = nrm((DEPTH, L_CMP * HEAD_DIM, HEAD_DIM), (L_CMP * HEAD_DIM) ** -0.5)
    inp['nsa_cmp_w2k'] = nrm((DEPTH, HEAD_DIM, HEAD_DIM), HEAD_DIM ** -0.5)
    inp['nsa_cmp_w1v'] = nrm((DEPTH, L_CMP * HEAD_DIM, HEAD_DIM), (L_CMP * HEAD_DIM) ** -0.5)
    inp['nsa_cmp_w2v'] = nrm((DEPTH, HEAD_DIM, HEAD_DIM), HEAD_DIM ** -0.5)
    inp['rwkv_mu'] = jax.random.uniform(ks.pop(), (DEPTH, RWKV_COLS), f32, 0.0, 1.0)
    inp['rwkv_w0'] = jax.random.uniform(ks.pop(), (DEPTH, RWKV_WIDTH), f32, -3.0, 1.0)
    inp['rwkv_w2'] = nrm((DEPTH, LORA_W, RWKV_WIDTH), LORA_W ** -0.5)
    inp['rwkv_a0'] = nrm((DEPTH, RWKV_WIDTH), 0.1)
    inp['rwkv_a2'] = nrm((DEPTH, LORA_A, RWKV_WIDTH), LORA_A ** -0.5)
    inp['rwkv_g2'] = nrm((DEPTH, LORA_G, RWKV_WIDTH), LORA_G ** -0.5)
    inp['rwkv_k_k'] = 0.85 + nrm((DEPTH, RWKV_WIDTH), 0.02)
    inp['rwkv_k_a'] = gain((DEPTH, RWKV_WIDTH))
    inp['rwkv_r_k'] = nrm((DEPTH, RWKV_HEADS, HEAD_DIM), 0.1)
    inp['rwkv_ln_g'] = gain((DEPTH, RWKV_WIDTH))
    inp['rwkv_ln_b'] = nrm((DEPTH, RWKV_WIDTH), 0.02)
    inp['peer_w_q'] = nrm((DEPTH, D_MODEL, PEER_HEADS * D_KEY), D_MODEL ** -0.5)
    inp['peer_k1'] = nrm((DEPTH, PEER_HEADS, N_KEYS, D_KEY // 2), (D_KEY // 2) ** -0.5)
    inp['peer_k2'] = nrm((DEPTH, PEER_HEADS, N_KEYS, D_KEY // 2), (D_KEY // 2) ** -0.5)
    inp['peer_u'] = nrm((DEPTH, N_EXPERTS, D_MODEL), D_MODEL ** -0.5)
    inp['peer_v'] = nrm((DEPTH, N_EXPERTS, D_MODEL), 0.5)
    return inp


def reference(x_prompt, x_sample, cache_nsa_cmp, cache_nsa_sel, state_nsa_win, state_gdn, state_gdn_conv,
              state_rwkv, state_rwkv_shift, page_table, c_prompt, c_sample,
              norm_mix_g, norm_ffn_g, norm_final_g, w_ada, b_ada, w_in, w_out,
              gdn_conv_w, gdn_A_log, gdn_dt_bias, gdn_norm_g,
              nsa_cmp_pos, nsa_cmp_w1k, nsa_cmp_w2k, nsa_cmp_w1v, nsa_cmp_w2v,
              rwkv_mu, rwkv_w0, rwkv_w2, rwkv_a0, rwkv_a2, rwkv_g2, rwkv_k_k, rwkv_k_a, rwkv_r_k,
              rwkv_ln_g, rwkv_ln_b,
              peer_w_q, peer_k1, peer_k2, peer_u, peer_v):
    p = dict(norm_mix_g=norm_mix_g, norm_ffn_g=norm_ffn_g, norm_final_g=norm_final_g, w_ada=w_ada, b_ada=b_ada,
             w_in=w_in, w_out=w_out, gdn_conv_w=gdn_conv_w, gdn_A_log=gdn_A_log, gdn_dt_bias=gdn_dt_bias,
             gdn_norm_g=gdn_norm_g, nsa_cmp_pos=nsa_cmp_pos, nsa_cmp_w1k=nsa_cmp_w1k, nsa_cmp_w2k=nsa_cmp_w2k,
             nsa_cmp_w1v=nsa_cmp_w1v, nsa_cmp_w2v=nsa_cmp_w2v, rwkv_mu=rwkv_mu, rwkv_w0=rwkv_w0, rwkv_w2=rwkv_w2,
             rwkv_a0=rwkv_a0, rwkv_a2=rwkv_a2, rwkv_g2=rwkv_g2, rwkv_k_k=rwkv_k_k, rwkv_k_a=rwkv_k_a,
             rwkv_r_k=rwkv_r_k, rwkv_ln_g=rwkv_ln_g, rwkv_ln_b=rwkv_ln_b, peer_w_q=peer_w_q, peer_k1=peer_k1,
             peer_k2=peer_k2, peer_u=peer_u, peer_v=peer_v)
    past_len = page_table.shape[1] * PAGE_SIZE
    pos_p = jnp.arange(x_prompt.shape[1], dtype=jnp.int32)
    pos_s = past_len + jnp.arange(x_sample.shape[1], dtype=jnp.int32)
    y_prompt, new_p = trunk(x_prompt, c_prompt, pos_p, p, None, None)
    y_sample, new_s = trunk(x_sample, c_sample, pos_s, p,
                            (state_gdn, state_gdn_conv, state_rwkv, state_rwkv_shift, state_nsa_win),
                            (cache_nsa_cmp, cache_nsa_sel, page_table))
    return (y_prompt, y_sample, new_p[0], new_p[1], new_p[2], new_p[3], new_p[4], new_p[5], new_p[6],
            new_s[0], new_s[1], new_s[2], new_s[3], new_s[4], new_s[5], new_s[6])
```

```python
import functools
import math

import numpy as np
import jax
import jax.numpy as jnp
from jax import lax
from jax.experimental import pallas as pl
from jax.experimental.pallas import tpu as pltpu

F32 = jnp.float32
BF16 = jnp.bfloat16
HI = lax.Precision.HIGHEST

D_MODEL = 1024
HEAD_DIM = 64
GDN_HEADS = 4
GDN_WIDTH = GDN_HEADS * HEAD_DIM
CONV_W = 4
CHUNK = 64
NSA_HEADS = 8
NSA_KV = 2
NSA_REP = 4
NSA_WIDTH = NSA_HEADS * HEAD_DIM
NSA_KVW = NSA_KV * HEAD_DIM
L_CMP = 32
STRIDE_CMP = 16
L_SEL = 64
TOP_SEL = 16
WINDOW = 512
NSA_QBLK = 128
ROT_DIM = 16
ROPE_THETA = 500000.0
RWKV_HEADS = 4
RWKV_WIDTH = RWKV_HEADS * HEAD_DIM
LORA_W = 64
LORA_A = 64
LORA_G = 128
GDN_COLS = 4 * GDN_WIDTH + 2 * GDN_HEADS
NSA_COLS = NSA_WIDTH + 6 * NSA_KVW + 3 * NSA_HEADS
RWKV_COLS = 3 * RWKV_WIDTH + LORA_W + LORA_A + LORA_G
N_KEYS = 128
N_EXPERTS = N_KEYS * N_KEYS
PEER_HEADS = 8
PEER_TOPK = 16
D_KEY = 256
PAGE_SIZE = 128
RMS_EPS = 1e-6
GN_EPS = 64e-5
NEG_INF = -1e30
FORCE = 1e4
PAD_SCORE = -3e38

LANES = 128
VMEM_LIMIT = 56 * 1024 * 1024
PEER_TM = 256
PEER_CHUNK_I1 = 16
W_PITCH = 136
SEL_TK = 512


def _cp(*sem):
    return pltpu.CompilerParams(dimension_semantics=sem, vmem_limit_bytes=VMEM_LIMIT)


def _dot(a, b, prec=None):
    return jnp.dot(a, b, preferred_element_type=F32, precision=prec)


def _dot_nt(a, b, prec=None):
    return lax.dot_general(a, b, (((1,), (1,)), ((), ())), preferred_element_type=F32, precision=prec)


def _dot_tn(a, b, prec=None):
    return lax.dot_general(a, b, (((0,), (0,)), ((), ())), preferred_element_type=F32, precision=prec)


def _silu(x):
    return x * jax.nn.sigmoid(x)


def _softplus(x):
    return jnp.maximum(x, 0.0) + jnp.log1p(jnp.exp(-jnp.abs(x)))


def _gelu(x):
    return 0.5 * x * (1.0 + lax.erf(x * np.float32(math.sqrt(0.5))))


def _iota(shape, dim):
    return lax.broadcasted_iota(jnp.int32, shape, dim)


def _tri_solve(L, rhs):
    x = rhs - _dot(L, rhs, HI)
    p = L
    n = L.shape[0]
    k = 2
    while k < n:
        p = _dot(p, p, HI)
        x = x + _dot(p, x, HI)
        k *= 2
    return x


def _ada_kernel(c_ref, w_ref, b_ref, o_ref):
    s = _silu(c_ref[...]).astype(BF16)
    o_ref[0] = _dot(s, w_ref[0].astype(BF16)) + b_ref[0]


def ada_all(c, w_ada, b_ada):
    depth, d, n = w_ada.shape
    r = c.shape[0]
    tn = 1536
    return pl.pallas_call(
        _ada_kernel,
        grid=(depth, n // tn),
        in_specs=[pl.BlockSpec((r, d), lambda l, j: (0, 0)),
                  pl.BlockSpec((1, d, tn), lambda l, j: (l, 0, j)),
                  pl.BlockSpec((1, 1, tn), lambda l, j: (l, 0, j))],
        out_specs=pl.BlockSpec((1, r, tn), lambda l, j: (l, 0, j)),
        out_shape=jax.ShapeDtypeStruct((depth, r, n), F32),
        compiler_params=_cp("arbitrary", "arbitrary"),
        name="ada",
    )(c, w_ada, b_ada.reshape(depth, 1, n))


def _mod_spec(mod, tm):
    if mod.shape[1] == 1:
        return pl.BlockSpec((1, 1, mod.shape[2]), lambda b, i: (b, 0, 0))
    return pl.BlockSpec((1, tm, mod.shape[2]), lambda b, i: (b, i, 0))


def _lnmm_kernel(x_ref, g_ref, sc_ref, sh_ref, w_ref, *out_refs, segs, emit_h):
    x = x_ref[0]
    ms = jnp.mean(x * x, axis=-1, keepdims=True)
    y = x * lax.rsqrt(ms + RMS_EPS) * g_ref[...]
    hb = (y * (1.0 + sc_ref[0]) + sh_ref[0]).astype(BF16)
    off = 0
    for o_ref, w in zip(out_refs, segs):
        o_ref[0] = _dot(hb, w_ref[:, off:off + w])
        off += w
    if emit_h:
        out_refs[-1][0] = hb


def ln_mod_matmul(x, gain, sc, sh, w, segs, tm, emit_h=False, name="lnmm"):
    b, t, d = x.shape
    out_shape = [jax.ShapeDtypeStruct((b, t, s), F32) for s in segs]
    out_specs = [pl.BlockSpec((1, tm, s), lambda bb, i: (bb, i, 0)) for s in segs]
    if emit_h:
        out_shape.append(jax.ShapeDtypeStruct((b, t, d), BF16))
        out_specs.append(pl.BlockSpec((1, tm, d), lambda bb, i: (bb, i, 0)))
    return pl.pallas_call(
        functools.partial(_lnmm_kernel, segs=tuple(segs), emit_h=emit_h),
        grid=(b, t // tm),
        in_specs=[pl.BlockSpec((1, tm, d), lambda bb, i: (bb, i, 0)),
                  pl.BlockSpec((1, d), lambda bb, i: (0, 0)),
                  _mod_spec(sc, tm), _mod_spec(sh, tm),
                  pl.BlockSpec(w.shape, lambda bb, i: (0, 0))],
        out_specs=out_specs,
        out_shape=out_shape,
        compiler_params=_cp("arbitrary", "arbitrary"),
        name=name,
    )(x, gain.reshape(1, d), sc, sh, w)


def _outproj_kernel(ya_ref, yb_ref, yc_ref, w_ref, x_ref, g_ref, o_ref):
    y = jnp.concatenate([ya_ref[0], yb_ref[0], yc_ref[0]], axis=-1).astype(BF16)
    o_ref[0] = x_ref[0] + g_ref[0] * _dot(y, w_ref[...])


def out_proj(ya, yb, yc, w, x, gate, tm):
    b, t, d = x.shape
    row = lambda n: pl.BlockSpec((1, tm, n), lambda bb, i: (bb, i, 0))
    return pl.pallas_call(
        _outproj_kernel,
        grid=(b, t // tm),
        in_specs=[row(ya.shape[2]), row(yb.shape[2]), row(yc.shape[2]),
                  pl.BlockSpec(w.shape, lambda bb, i: (0, 0)), row(d), _mod_spec(gate, tm)],
        out_specs=row(d),
        out_shape=jax.ShapeDtypeStruct((b, t, d), F32),
        compiler_params=_cp("arbitrary", "arbitrary"),
        name="out_proj",
    )(ya, yb, yc, w, x, gate)


def _final_norm_kernel(x_ref, g_ref, o_ref):
    x = x_ref[0]
    ms = jnp.mean(x * x, axis=-1, keepdims=True)
    o_ref[0] = x * lax.rsqrt(ms + RMS_EPS) * g_ref[...]


def final_norm(x, gain, tm):
    b, t, d = x.shape
    return pl.pallas_call(
        _final_norm_kernel,
        grid=(b, t // tm),
        in_specs=[pl.BlockSpec((1, tm, d), lambda bb, i: (bb, i, 0)), pl.BlockSpec((1, d), lambda bb, i: (0, 0))],
        out_specs=pl.BlockSpec((1, tm, d), lambda bb, i: (bb, i, 0)),
        out_shape=jax.ShapeDtypeStruct((b, t, d), F32),
        compiler_params=_cp("arbitrary", "arbitrary"),
        name="final_norm",
    )(x, gain.reshape(1, d))


def _topk_rows(x, k):
    n = x.shape[0]
    iota = _iota(x.shape, 0)
    vals, idxs = [], []
    for _ in range(k):
        m = jnp.max(x, axis=0, keepdims=True)
        idx = jnp.min(jnp.where(x == m, iota, n), axis=0, keepdims=True)
        vals.append(m)
        idxs.append(idx)
        x = jnp.where(iota == idx, -jnp.inf, x)
    return jnp.concatenate(vals, axis=0), jnp.concatenate(idxs, axis=0)


def _peer_route_kernel(q_ref, k1_ref, k2_ref, i1_ref, i2_ref, gw_ref):
    q = q_ref[0].astype(BF16)
    half = D_KEY // 2
    s1 = _dot_nt(k1_ref[0], q[:, :half])
    s2 = _dot_nt(k2_ref[0], q[:, half:])
    v1, i1 = _topk_rows(s1, PEER_TOPK)
    v2, i2 = _topk_rows(s2, PEER_TOPK)
    tm = q.shape[0]
    cand = jnp.concatenate([v1[a:a + 1] + v2 for a in range(PEER_TOPK)], axis=0)
    c1 = jnp.concatenate([jnp.broadcast_to(i1[a:a + 1], (PEER_TOPK, tm)) for a in range(PEER_TOPK)], axis=0)
    c2 = jnp.concatenate([i2] * PEER_TOPK, axis=0)
    iota = _iota(cand.shape, 0)
    n = cand.shape[0]
    sc, e1, e2 = [], [], []
    for _ in range(PEER_TOPK):
        m = jnp.max(cand, axis=0, keepdims=True)
        pos = jnp.min(jnp.where(cand == m, iota, n), axis=0, keepdims=True)
        oh = iota == pos
        sc.append(m)
        e1.append(jnp.sum(jnp.where(oh, c1, 0), axis=0, keepdims=True))
        e2.append(jnp.sum(jnp.where(oh, c2, 0), axis=0, keepdims=True))
        cand = jnp.where(oh, -jnp.inf, cand)
    sc = jnp.concatenate(sc, axis=0)
    p = jnp.exp(sc - sc[0:1])
    gw_ref[0, 0] = p / jnp.sum(p, axis=0, keepdims=True)
    i1_ref[0, 0] = jnp.concatenate(e1, axis=0).astype(F32)
    i2_ref[0, 0] = jnp.concatenate(e2, axis=0).astype(F32)


def peer_route(q, k1, k2, tm):
    b, t, _ = q.shape
    nb = t // tm
    npair = PEER_HEADS * PEER_TOPK
    out = jax.ShapeDtypeStruct((b, nb, npair, tm), F32)
    ospec = pl.BlockSpec((1, 1, PEER_TOPK, tm), lambda bb, i, h: (bb, i, h, 0))
    return pl.pallas_call(
        _peer_route_kernel,
        grid=(b, nb, PEER_HEADS),
        in_specs=[pl.BlockSpec((1, tm, D_KEY), lambda bb, i, h: (bb, i, h)),
                  pl.BlockSpec((1, N_KEYS, D_KEY // 2), lambda bb, i, h: (h, 0, 0)),
                  pl.BlockSpec((1, N_KEYS, D_KEY // 2), lambda bb, i, h: (h, 0, 0))],
        out_specs=[ospec, ospec, ospec],
        out_shape=[out, out, out],
        compiler_params=_cp("arbitrary", "arbitrary", "arbitrary"),
        name="peer_route",
    )(q, k1, k2)


def _peer_expert_kernel(h_ref, i1_ref, i2_ref, gw_ref, ut_ref, v_ref, x_ref, g_ref, o_ref,
                        wscr, acc, i1t, i2t, gwt):
    c = pl.program_id(2)
    tm = h_ref.shape[1]
    npair = i1_ref.shape[2]

    @pl.when(c == 0)
    def _():
        acc[...] = jnp.zeros_like(acc)
        i1t[...] = i1_ref[0, 0].T
        i2t[...] = i2_ref[0, 0].T
        gwt[...] = gw_ref[0, 0].T
        sub = _iota((N_KEYS, npair), 0).astype(F32)

        def tok(t, carry):
            r1 = i1t[pl.ds(t, 1), :]
            r2 = i2t[pl.ds(t, 1), :]
            gw = gwt[pl.ds(t, 1), :]
            hi = gw.astype(BF16).astype(F32)
            lo = gw - hi
            m1 = sub == r1
            lhs = jnp.concatenate([jnp.where(m1, hi, 0.0), jnp.where(m1, lo, 0.0)], axis=0).astype(BF16)
            bt = jnp.where(sub == r2, 1.0, 0.0).astype(BF16)
            w2 = _dot_nt(lhs, bt)
            wscr[pl.ds(pl.multiple_of(t * W_PITCH, 8), N_KEYS), :] = w2[:N_KEYS] + w2[N_KEYS:]
            return carry

        lax.fori_loop(0, tm, tok, 0)

    act = _gelu(_dot(h_ref[0], ut_ref[...]))
    parts = []
    for j in range(PEER_CHUNK_I1):
        wi = wscr[pl.ds(c * PEER_CHUNK_I1 + j, tm, stride=W_PITCH), :]
        parts.append((act[:, j * N_KEYS:(j + 1) * N_KEYS] * wi).astype(BF16))
    acc[...] += _dot(jnp.concatenate(parts, axis=1), v_ref[...])

    @pl.when(c == pl.num_programs(2) - 1)
    def _():
        o_ref[0] = x_ref[0] + g_ref[0] * acc[...]


def peer_experts(hb, i1, i2, gw, ut, v, x, gate, tm):
    b, t, d = x.shape
    ce = PEER_CHUNK_I1 * N_KEYS
    nch = N_EXPERTS // ce
    npair = PEER_HEADS * PEER_TOPK
    rspec = pl.BlockSpec((1, 1, npair, tm), lambda bb, i, c: (bb, i, 0, 0))
    gspec = (pl.BlockSpec((1, 1, d), lambda bb, i, c: (bb, 0, 0)) if gate.shape[1] == 1
             else pl.BlockSpec((1, tm, d), lambda bb, i, c: (bb, i, 0)))
    return pl.pallas_call(
        _peer_expert_kernel,
        grid=(b, t // tm, nch),
        in_specs=[pl.BlockSpec((1, tm, d), lambda bb, i, c: (bb, i, 0)), rspec, rspec, rspec,
                  pl.BlockSpec((d, ce), lambda bb, i, c: (0, c)),
                  pl.BlockSpec((ce, d), lambda bb, i, c: (c, 0)),
                  pl.BlockSpec((1, tm, d), lambda bb, i, c: (bb, i, 0)), gspec],
        out_specs=pl.BlockSpec((1, tm, d), lambda bb, i, c: (bb, i, 0)),
        out_shape=jax.ShapeDtypeStruct((b, t, d), F32),
        scratch_shapes=[pltpu.VMEM((tm * W_PITCH, N_KEYS), F32), pltpu.VMEM((tm, d), F32),
                        pltpu.VMEM((tm, npair), F32), pltpu.VMEM((tm, npair), F32), pltpu.VMEM((tm, npair), F32)],
        compiler_params=_cp("arbitrary", "arbitrary", "arbitrary"),
        name="peer_experts",
    )(hb, i1, i2, gw, ut, v, x, gate)


def peer_block(x, gain, sc, sh, gate, w_q, k1, k2, ut, v, tm):
    q, hb = ln_mod_matmul(x, gain, sc, sh, w_q, [PEER_HEADS * D_KEY], tm, emit_h=True, name="peer_query")
    i1, i2, gw = peer_route(q, k1, k2, tm)
    return peer_experts(hb, i1, i2, gw, ut, v, x, gate, tm)


def _tri_masks(n):
    r, c = _iota((n, n), 0), _iota((n, n), 1)
    return r >= c, r > c


def _l2norm(x):
    return x * lax.rsqrt(jnp.sum(x * x, axis=-1, keepdims=True) + RMS_EPS)


def _head(x, h):
    return x[:, h * HEAD_DIM:(h + 1) * HEAD_DIM]


def _gdn_seq_kernel(qkv_ref, sm_ref, gate_ref, cw_ref, alog_ref, dtb_ref, ng_ref, ya_ref, st_ref, s_scr, xbuf):
    c = pl.program_id(1)
    n = CHUNK
    w3 = 3 * GDN_WIDTH

    @pl.when(c == 0)
    def _():
        s_scr[...] = jnp.zeros_like(s_scr)
        xbuf[0:8, :] = jnp.zeros((8, w3), F32)

    x = qkv_ref[0]
    xbuf[8:8 + n, :] = x
    y = cw_ref[0:1, :] * xbuf[5:5 + n, :]
    y = y + cw_ref[1:2, :] * xbuf[6:6 + n, :]
    y = y + cw_ref[2:3, :] * xbuf[7:7 + n, :]
    y = _silu(y + cw_ref[3:4, :] * x)
    xbuf[0:8, :] = xbuf[n:n + 8, :]

    sm = sm_ref[0]
    g = -jnp.exp(alog_ref[...]) * _softplus(sm + dtb_ref[...])
    beta = jax.nn.sigmoid(sm)
    incl, strict = _tri_masks(n)
    gcum = _dot(jnp.where(incl, 1.0, 0.0), g, HI)
    gcum_t = gcum.T
    gate = gate_ref[0]
    outs = []
    for h in range(GDN_HEADS):
        gc = gcum[:, h:h + 1]
        gr = gcum_t[h:h + 1, :]
        decay = jnp.where(incl, jnp.exp(jnp.where(incl, gc - gr, 0.0)), 0.0)
        q = _l2norm(_head(y, h)) * (HEAD_DIM ** -0.5)
        k = _l2norm(_head(y, GDN_HEADS + h))
        v = _head(y, 2 * GDN_HEADS + h)
        b = beta[:, GDN_HEADS + h:GDN_HEADS + h + 1]
        kb = k * b
        eg = jnp.exp(gc)
        kk = _dot_nt(jnp.concatenate([kb, q], axis=0), k, HI)
        lmat = jnp.where(strict, kk[:n] * decay, 0.0)
        att = kk[n:] * decay
        sol = _tri_solve(lmat, jnp.concatenate([v * b, kb * eg], axis=1))
        u, wk = sol[:, :HEAD_DIM], sol[:, HEAD_DIM:]
        glast = gc[n - 1:n, :]
        kdec = k * jnp.exp(glast - gc)
        s0 = s_scr[h]
        xs = _dot(jnp.concatenate([wk, q * eg], axis=0), s0, HI)
        v_new = u - xs[:n]
        o = xs[n:] + _dot(att, v_new, HI)
        s_scr[h] = s0 * jnp.exp(glast) + _dot_tn(kdec, v_new, HI)
        ms = jnp.mean(o * o, axis=-1, keepdims=True)
        outs.append(o * lax.rsqrt(ms + RMS_EPS) * ng_ref[...] * _silu(_head(gate, h)))
    ya_ref[0] = jnp.concatenate(outs, axis=-1)

    @pl.when(c == pl.num_programs(1) - 1)
    def _():
        st_ref[0] = s_scr[...]


def _pad_lanes(v, n=LANES):
    v = v.reshape(1, -1)
    return jnp.pad(v, ((0, 0), (0, n - v.shape[1])))


def gdn_seq(qkv, small, gate, conv_w, a_log, dt_bias, norm_g):
    b, t, w3 = qkv.shape
    full = lambda a: pl.BlockSpec(a.shape, lambda bb, c: (0,) * a.ndim)
    alog = _pad_lanes(a_log)
    dtb = _pad_lanes(dt_bias)
    ng = norm_g.reshape(1, HEAD_DIM)
    return pl.pallas_call(
        _gdn_seq_kernel,
        grid=(b, t // CHUNK),
        in_specs=[pl.BlockSpec((1, CHUNK, w3), lambda bb, c: (bb, c, 0)),
                  pl.BlockSpec((1, CHUNK, LANES), lambda bb, c: (bb, c, 0)),
                  pl.BlockSpec((1, CHUNK, GDN_WIDTH), lambda bb, c: (bb, c, 0)),
                  full(conv_w), full(alog), full(dtb), full(ng)],
        out_specs=[pl.BlockSpec((1, CHUNK, GDN_WIDTH), lambda bb, c: (bb, c, 0)),
                   pl.BlockSpec((1, GDN_HEADS, HEAD_DIM, HEAD_DIM), lambda bb, c: (bb, 0, 0, 0))],
        out_shape=[jax.ShapeDtypeStruct((b, t, GDN_WIDTH), F32),
                   jax.ShapeDtypeStruct((b, GDN_HEADS, HEAD_DIM, HEAD_DIM), F32)],
        scratch_shapes=[pltpu.VMEM((GDN_HEADS, HEAD_DIM, HEAD_DIM), F32), pltpu.VMEM((CHUNK + 8, w3), F32)],
        compiler_params=_cp("arbitrary", "arbitrary"),
        name="gdn_seq",
    )(qkv, small, gate, conv_w, alog, dtb, ng)


def _rwkv_mix(xs, w0, w2, a0, a2, g2, k_k, k_a):
    w = RWKV_WIDTH
    r, k, v = xs[:, :w], xs[:, w:2 * w], xs[:, 2 * w:3 * w]
    xw = xs[:, 3 * w:3 * w + LORA_W]
    xa = xs[:, 3 * w + LORA_W:3 * w + LORA_W + LORA_A]
    xg = xs[:, 3 * w + LORA_W + LORA_A:]
    wlog = -_softplus(-(w0 + _dot(jnp.tanh(xw).astype(BF16), w2))) - 0.5
    e = jnp.exp(wlog)
    a = jax.nn.sigmoid(a0 + _dot(xa.astype(BF16), a2))
    gout = _dot(jax.nn.sigmoid(xg).astype(BF16), g2)
    kkf = k * k_k
    kk = jnp.concatenate([_l2norm(_head(kkf, h)) for h in range(RWKV_HEADS)], axis=-1)
    k = k * (1.0 + (a - 1.0) * k_a)
    return r, e, k, v, kk, a, gout


def _rwkv_out(o, r, k, v, gout, r_k, ln_g, ln_b):
    outs = []
    for h in range(RWKV_HEADS):
        oh = _head(o, h)
        mean = jnp.mean(oh, axis=-1, keepdims=True)
        var = jnp.mean(jnp.square(oh - mean), axis=-1, keepdims=True)
        on = (oh - mean) * lax.rsqrt(var + GN_EPS) * _head(ln_g, h) + _head(ln_b, h)
        bonus = jnp.sum(_head(r, h) * _head(k, h) * _head(r_k, h), axis=-1, keepdims=True) * _head(v, h)
        outs.append((on + bonus) * _head(gout, h))
    return jnp.concatenate(outs, axis=-1)


def _rwkv_seq_kernel(cols_ref, mu_ref, w0_ref, w2_ref, a0_ref, a2_ref, g2_ref, kk_ref, ka_ref, rk_ref,
                     lng_ref, lnb_ref, yc_ref, st_ref, s_scr, xbuf):
    c = pl.program_id(1)
    n = CHUNK

    @pl.when(c == 0)
    def _():
        s_scr[...] = jnp.zeros_like(s_scr)
        xbuf[0:8, :] = jnp.zeros((8, RWKV_COLS), F32)

    x = cols_ref[0]
    xbuf[8:8 + n, :] = x
    prev = xbuf[7:7 + n, :]
    xbuf[0:8, :] = xbuf[n:n + 8, :]
    xs = x + (prev - x) * mu_ref[...]
    r, e, k, v, kk, a, gout = _rwkv_mix(xs, w0_ref[...], w2_ref[...], a0_ref[...], a2_ref[...], g2_ref[...],
                                        kk_ref[...], ka_ref[...])
    incl, strict = _tri_masks(n)
    gc_all = -_dot(jnp.where(incl, 1.0, 0.0), e, HI)
    outs = []
    for h in range(RWKV_HEADS):
        gc = _head(gc_all, h)
        gprev = gc + _head(e, h)
        glast = gc[n - 1:n, :]
        rh, kh, vh, kkh = _head(r, h), _head(k, h), _head(v, h), _head(kk, h)
        bh = kkh * _head(a, h)
        einv = jnp.exp(-gc)
        kap = kkh * jnp.exp(gprev)
        rt = rh * jnp.exp(gc)
        etail = jnp.exp(glast - gc)
        m = _dot_nt(jnp.concatenate([kap, rt], axis=0), jnp.concatenate([bh * einv, kh * einv], axis=0), HI)
        lb = jnp.where(strict, m[:n, :n], 0.0)
        lk = jnp.where(strict, m[:n, n:], 0.0)
        ab = jnp.where(incl, m[n:, :n], 0.0)
        ak = jnp.where(incl, m[n:, n:], 0.0)
        sol = _tri_solve(lb, jnp.concatenate([kap, _dot(lk, vh, HI)], axis=1))
        s0 = s_scr[h]
        xt = _dot_nt(jnp.concatenate([sol[:, :HEAD_DIM], rt], axis=0), s0, HI)
        u = sol[:, HEAD_DIM:] + xt[:n]
        outs.append(xt[n:] + _dot(ak, vh, HI) - _dot(ab, u, HI))
        s_scr[h] = s0 * jnp.exp(glast) + _dot_tn(vh, kh * etail, HI) - _dot_tn(u, bh * etail, HI)
    o = jnp.concatenate(outs, axis=-1)
    yc_ref[0] = _rwkv_out(o, r, k, v, gout, rk_ref[...], lng_ref[...], lnb_ref[...])

    @pl.when(c == pl.num_programs(1) - 1)
    def _():
        st_ref[0] = s_scr[...]


def rope_tables(pos):
    half = ROT_DIM // 2
    inv = ROPE_THETA ** (-jnp.arange(half, dtype=F32) / half)
    ang = pos.astype(F32)[:, None] * inv[None, :]
    cos, sin = jnp.cos(ang), jnp.sin(ang)
    n = pos.shape[0]
    one = jnp.ones((n, HEAD_DIM - ROT_DIM), F32)
    zero = jnp.zeros((n, HEAD_DIM - ROT_DIM), F32)
    z8 = jnp.zeros((n, half), F32)
    tc = jnp.concatenate([cos, cos, one], axis=1)
    t1 = jnp.concatenate([-sin, z8, zero], axis=1)
    t2 = jnp.concatenate([z8, sin, zero], axis=1)
    rep = LANES // HEAD_DIM
    return jnp.tile(tc, (1, rep)), jnp.tile(t1, (1, rep)), jnp.tile(t2, (1, rep))


def _rope(x, tc, t1, t2):
    half = ROT_DIM // 2
    return x * tc + pltpu.roll(x, LANES - half, 1) * t1 + pltpu.roll(x, half, 1) * t2


def _nsa_prep_kernel(q_ref, kv_ref, tc_ref, t1_ref, t2_ref, qo_ref, cmp_ref, sel_ref, win_ref, selb_ref, winb_ref):
    tc, t1, t2 = tc_ref[...], t1_ref[...], t2_ref[...]
    q = q_ref[0]
    scale = HEAD_DIM ** -0.5
    qo_ref[0] = jnp.concatenate(
        [_rope(q[:, j * LANES:(j + 1) * LANES], tc, t1, t2) * scale for j in range(NSA_WIDTH // LANES)],
        axis=-1).astype(BF16)
    kv = kv_ref[0]
    for br, (o_ref, ob_ref) in enumerate(((cmp_ref, None), (sel_ref, selb_ref), (win_ref, winb_ref))):
        k = _rope(kv[:, (2 * br) * LANES:(2 * br + 1) * LANES], tc, t1, t2)
        v = kv[:, (2 * br + 1) * LANES:(2 * br + 2) * LANES]
        o_ref[0, 0] = k
        o_ref[0, 1] = v
        if ob_ref is not None:
            ob_ref[0, 0] = k.astype(BF16)
            ob_ref[0, 1] = v.astype(BF16)


def nsa_prep(qc, kvc, tables, tm):
    b, t, _ = qc.shape
    kvs = jax.ShapeDtypeStruct((b, 2, t, LANES), F32)
    kvb = jax.ShapeDtypeStruct((b, 2, t, LANES), BF16)
    kspec = pl.BlockSpec((1, 2, tm, LANES), lambda bb, i: (bb, 0, i, 0))
    tspec = pl.BlockSpec((tm, LANES), lambda bb, i: (i, 0))
    return pl.pallas_call(
        _nsa_prep_kernel,
        grid=(b, t // tm),
        in_specs=[pl.BlockSpec((1, tm, NSA_WIDTH), lambda bb, i: (bb, i, 0)),
                  pl.BlockSpec((1, tm, 6 * LANES), lambda bb, i: (bb, i, 0)), tspec, tspec, tspec],
        out_specs=[pl.BlockSpec((1, tm, NSA_WIDTH), lambda bb, i: (bb, i, 0)), kspec, kspec, kspec, kspec, kspec],
        out_shape=[jax.ShapeDtypeStruct((b, t, NSA_WIDTH), BF16), kvs, kvs, kvs, kvb, kvb],
        compiler_params=_cp("arbitrary", "arbitrary"),
        name="nsa_prep",
    )(qc, kvc, *tables)


def _compress_rows(r16, w1ab, pe16, w2):
    pq = _dot(r16.astype(BF16), w1ab)
    n16 = pq.shape[0]
    nxt = pltpu.roll(pq[:, LANES:], n16 - 1, 0)
    pqe = _dot(pe16, w1ab.astype(F32), HI)
    bias = pqe[0:1, :LANES] + pqe[1:2, LANES:]
    return _dot(_gelu(pq[:, :LANES] + nxt + bias).astype(BF16), w2)


def _nsa_compress_kernel(x_ref, w1_ref, pe_ref, w2_ref, o_ref):
    o_ref[0, 0] = _compress_rows(x_ref[0, 0], w1_ref[0], pe_ref[...], w2_ref[0])


def compress_weights(pe, w1k, w2k, w1v, w2v):
    eye = jnp.eye(NSA_KV, dtype=F32)

    def one(w1, w2):
        w1r = w1.reshape(L_CMP, HEAD_DIM, HEAD_DIM)
        bd = (w1r[:, None, :, None, :] * eye[None, :, None, :, None]).reshape(L_CMP * LANES, LANES)
        half = STRIDE_CMP * LANES
        w1ab = jnp.concatenate([bd[:half], bd[half:]], axis=1)
        w2bd = (w2[None, :, None, :] * eye[:, None, :, None]).reshape(LANES, LANES)
        return w1ab.astype(BF16), w2bd.astype(BF16)

    k, v = one(w1k, w2k), one(w1v, w2v)
    pe16 = jnp.tile(pe, (1, NSA_KV)).reshape(L_CMP // STRIDE_CMP, STRIDE_CMP * LANES)
    pe16 = jnp.pad(pe16, ((0, 8 - pe16.shape[0]), (0, 0)))
    return jnp.stack([k[0], v[0]]), pe16, jnp.stack([k[1], v[1]])


def nsa_compress(kv16, cw):
    b, _, n16, w = kv16.shape
    w1ab, pe16, w2 = cw
    return pl.pallas_call(
        _nsa_compress_kernel,
        grid=(b, 2),
        in_specs=[pl.BlockSpec((1, 1, n16, w), lambda bb, j: (bb, j, 0, 0)),
                  pl.BlockSpec((1,) + w1ab.shape[1:], lambda bb, j: (j, 0, 0)),
                  pl.BlockSpec(pe16.shape, lambda bb, j: (0, 0)),
                  pl.BlockSpec((1, LANES, LANES), lambda bb, j: (j, 0, 0))],
        out_specs=pl.BlockSpec((1, 1, n16, LANES), lambda bb, j: (bb, j, 0, 0)),
        out_shape=jax.ShapeDtypeStruct((b, 2, n16, LANES), F32),
        compiler_params=_cp("arbitrary", "arbitrary"),
        name="nsa_compress",
    )(kv16, w1ab, pe16, w2)


def selection_counts(n_cmp_pad, n_sel, n_sel_pad):
    i = np.arange(n_cmp_pad)[:, None, None, None]
    j = np.arange(n_sel)[None, :, None, None]
    m = np.arange(L_SEL // STRIDE_CMP)[None, None, :, None]
    n = np.arange(L_CMP // STRIDE_CMP)[None, None, None, :]
    cnt = np.sum(i == (L_SEL // STRIDE_CMP) * j + m - n, axis=(2, 3)).astype(np.float32)
    return jnp.asarray(np.pad(cnt, ((0, 0), (0, n_sel_pad - n_sel))))


def _topk_mask_lanes(x, k):
    iota = _iota(x.shape, 1)
    big = x.shape[1]
    sel = jnp.zeros(x.shape, F32)
    for _ in range(k):
        m = jnp.max(x, axis=-1, keepdims=True)
        idx = jnp.min(jnp.where(x == m, iota, big), axis=-1, keepdims=True)
        oh = iota == idx
        sel = jnp.where(oh, 1.0, sel)
        x = jnp.where(oh, -jnp.inf, x)
    return sel


def _masked_softmax(s, mask):
    sm = jnp.where(mask, s, NEG_INF)
    p = jnp.exp(sm - jnp.max(sm, axis=-1, keepdims=True))
    p = p / jnp.sum(p, axis=-1, keepdims=True)
    return jnp.where(mask, p, 0.0)


def _select_blocks(pcs, cnt, qpos_col, n_sel):
    imp = _dot(pcs, cnt, HI)
    blk = _iota(imp.shape, 1)
    cur = qpos_col // L_SEL
    imp = jnp.where(blk == 0, FORCE, jnp.where(blk == cur, FORCE, jnp.where(blk == cur - 1, FORCE, imp)))
    imp = jnp.where(blk > cur, -FORCE, imp)
    imp = jnp.where(blk >= n_sel, PAD_SCORE, imp)
    return _topk_mask_lanes(imp, min(TOP_SEL, n_sel))


def _expand_sel(sel, k0, tk):
    nsp = sel.shape[1]
    e = _iota((nsp, tk), 0) == (k0 + _iota((nsp, tk), 1)) // L_SEL
    return _dot(sel.astype(BF16), jnp.where(e, 1.0, 0.0).astype(BF16))


def _nsa_seq_kernel(q_ref, sm_ref, kc_ref, ks_ref, kw_ref, cnt_ref, y_ref, *, t_len, n_sel):
    i = pl.program_id(1)
    nq = NSA_QBLK
    rows = NSA_REP * nq
    s0 = i * nq
    qpos = s0 + _iota((rows, 1), 0) % nq
    qpos_q = s0 + _iota((nq, 1), 0)
    gates = jax.nn.sigmoid(sm_ref[0])
    n_cmp = kc_ref.shape[2]
    cend = _iota((1, n_cmp), 1) * STRIDE_CMP + (L_CMP - 1)
    wlen = WINDOW + nq
    ws = pl.multiple_of(jnp.maximum(s0 - WINDOW, 0), nq)
    wpos = ws + _iota((1, wlen), 1)
    n_tiles = (s0 + nq + SEL_TK - 1) // SEL_TK
    ys = []
    for g in range(NSA_KV):
        lo, hi = g * HEAD_DIM, (g + 1) * HEAD_DIM
        qg = q_ref[0][:, g * NSA_REP * HEAD_DIM:(g + 1) * NSA_REP * HEAD_DIM]
        qq = jnp.concatenate([_head(qg, r) for r in range(NSA_REP)], axis=0)
        kc = kc_ref[0, 0][:, lo:hi].astype(BF16)
        vc = kc_ref[0, 1][:, lo:hi].astype(BF16)
        pc = _masked_softmax(_dot_nt(qq, kc), cend <= qpos)
        o_c = _dot(pc.astype(BF16), vc)
        pcs = pc[0:nq]
        for r in range(1, NSA_REP):
            pcs = pcs + pc[r * nq:(r + 1) * nq]
        sel = _select_blocks(pcs, cnt_ref[...], qpos_q, n_sel)

        def tile(kt, carry):
            m, l, acc = carry
            k0 = pl.multiple_of(kt * SEL_TK, SEL_TK)
            kt_ = ks_ref[0, 0, pl.ds(k0, SEL_TK), :][:, lo:hi]
            vt_ = ks_ref[0, 1, pl.ds(k0, SEL_TK), :][:, lo:hi]
            s = _dot_nt(qq, kt_)
            me = _expand_sel(sel, k0, SEL_TK)
            me = jnp.concatenate([me] * NSA_REP, axis=0)
            kpos = k0 + _iota((1, SEL_TK), 1)
            s = jnp.where(kpos <= qpos, jnp.where(me > 0.5, s, NEG_INF), NEG_INF)
            m_new = jnp.maximum(m, jnp.max(s, axis=-1, keepdims=True))
            alpha = jnp.exp(m - m_new)
            p = jnp.where(s > 0.5 * NEG_INF, jnp.exp(s - m_new), 0.0)
            l = alpha * l + jnp.sum(p, axis=-1, keepdims=True)
            acc = alpha * acc + _dot(p.astype(BF16), vt_)
            return m_new, l, acc

        init = (jnp.full((rows, 1), NEG_INF, F32), jnp.zeros((rows, 1), F32), jnp.zeros((rows, HEAD_DIM), F32))
        _, l, acc = lax.fori_loop(0, n_tiles, tile, init)
        o_s = acc * jnp.where(l > 0.0, 1.0 / l, 0.0)
        kw = kw_ref[0, 0, pl.ds(ws, wlen), :][:, lo:hi]
        vw = kw_ref[0, 1, pl.ds(ws, wlen), :][:, lo:hi]
        mw = jnp.logical_and(wpos <= qpos, wpos > qpos - WINDOW)
        pw = _masked_softmax(_dot_nt(qq, kw), mw)
        o_w = _dot(pw.astype(BF16), vw)
        for r in range(NSA_REP):
            gl = 2 * GDN_HEADS + (g * NSA_REP + r) * 3
            gc, gs, gw = (gates[:, gl + j:gl + j + 1] for j in range(3))
            sl = slice(r * nq, (r + 1) * nq)
            ys.append(gc * o_c[sl] + gs * o_s[sl] + gw * o_w[sl])
    y_ref[0] = jnp.concatenate(ys, axis=-1)


def nsa_seq(q_rot, small, kvc_cmp, sel_b, win_b):
    b, t, _ = q_rot.shape
    assert t % SEL_TK == 0 and t >= WINDOW + NSA_QBLK
    n_sel = t // L_SEL
    n_sel_pad = -(-n_sel // LANES) * LANES
    n_cmp = kvc_cmp.shape[2]
    cnt = selection_counts(n_cmp, n_sel, n_sel_pad)
    whole = lambda a: pl.BlockSpec((1,) + a.shape[1:], lambda bb, i: (bb,) + (0,) * (a.ndim - 1))
    return pl.pallas_call(
        functools.partial(_nsa_seq_kernel, t_len=t, n_sel=n_sel),
        grid=(b, t // NSA_QBLK),
        in_specs=[pl.BlockSpec((1, NSA_QBLK, NSA_WIDTH), lambda bb, i: (bb, i, 0)),
                  pl.BlockSpec((1, NSA_QBLK, LANES), lambda bb, i: (bb, i, 0)),
                  whole(kvc_cmp), whole(sel_b), whole(win_b),
                  pl.BlockSpec(cnt.shape, lambda bb, i: (0, 0))],
        out_specs=pl.BlockSpec((1, NSA_QBLK, NSA_WIDTH), lambda bb, i: (bb, i, 0)),
        out_shape=jax.ShapeDtypeStruct((b, t, NSA_WIDTH), F32),
        compiler_params=_cp("arbitrary", "arbitrary"),
        name="nsa_seq",
    )(q_rot, small, kvc_cmp, sel_b, win_b, cnt)


def _nsa_decode_kernel(pt_ref, q_ref, sm_ref, snew_ref, wnew_ref, cc_ref, cs_ref, win_ref, w1_ref, pe_ref, w2_ref,
                       cnt_ref, y_ref, wout_ref, cbuf, sbuf, sem, *, layer, n_pages, n_sel):
    b = pl.program_id(0)
    past = n_pages * PAGE_SIZE
    rows16 = PAGE_SIZE // STRIDE_CMP

    def cmp_copy(page, p):
        return pltpu.make_async_copy(cc_ref.at[layer, page], cbuf.at[:, pl.ds(p * rows16, rows16), :], sem.at[0])

    def sel_copy(page, p):
        return pltpu.make_async_copy(cs_ref.at[layer, page], sbuf.at[:, pl.ds(p * PAGE_SIZE, PAGE_SIZE), :], sem.at[1])

    def issue(p, carry):
        page = pt_ref[b, p]
        cmp_copy(page, p).start()
        sel_copy(page, p).start()
        return carry

    def wait(p, carry):
        page = pt_ref[b, p]
        cmp_copy(page, p).wait()
        sel_copy(page, p).wait()
        return carry

    lax.fori_loop(0, n_pages, issue, 0)
    lax.fori_loop(0, n_pages, wait, 0)

    nh = NSA_HEADS
    qf = q_ref[0].astype(F32)
    zero = jnp.zeros((1, HEAD_DIM), F32)
    q8 = jnp.concatenate(
        [jnp.concatenate([_head(qf, j), zero] if j < NSA_REP else [zero, _head(qf, j)], axis=1) for j in range(nh)],
        axis=0)
    q8b = q8.astype(BF16)
    qpos = jnp.full((nh, 1), past, jnp.int32)

    kc = _compress_rows(cbuf[0], w1_ref[0], pe_ref[...], w2_ref[0])
    vc = _compress_rows(cbuf[1], w1_ref[1], pe_ref[...], w2_ref[1])
    n_cmp = kc.shape[0]
    cend = _iota((1, n_cmp), 1) * STRIDE_CMP + (L_CMP - 1)
    pc = _masked_softmax(_dot_nt(q8b, kc.astype(BF16)), cend <= qpos)
    o_c = _dot(pc.astype(BF16), vc.astype(BF16))
    pcs = jnp.concatenate([jnp.sum(pc[g * NSA_REP:(g + 1) * NSA_REP], axis=0, keepdims=True) for g in range(NSA_KV)]
                          + [jnp.zeros((nh - NSA_KV, n_cmp), F32)], axis=0)
    sel = _select_blocks(pcs, cnt_ref[...], qpos, n_sel)
    sel8 = jnp.concatenate([jnp.broadcast_to(sel[g:g + 1], (NSA_REP, sel.shape[1])) for g in range(NSA_KV)], axis=0)

    def tile(kt, carry):
        m, l, acc = carry
        k0 = pl.multiple_of(kt * SEL_TK, SEL_TK)
        s = _dot_nt(q8b, sbuf[0, pl.ds(k0, SEL_TK), :].astype(BF16))
        s = jnp.where(_expand_sel(sel8, k0, SEL_TK) > 0.5, s, NEG_INF)
        m_new = jnp.maximum(m, jnp.max(s, axis=-1, keepdims=True))
        alpha = jnp.exp(m - m_new)
        p = jnp.where(s > 0.5 * NEG_INF, jnp.exp(s - m_new), 0.0)
        l = alpha * l + jnp.sum(p, axis=-1, keepdims=True)
        acc = alpha * acc + _dot(p.astype(BF16), sbuf[1, pl.ds(k0, SEL_TK), :].astype(BF16))
        return m_new, l, acc

    def add_new(carry, k_row, v_row, valid):
        m, l, acc = carry
        kb = k_row.astype(BF16).astype(F32)
        vb = v_row.astype(BF16).astype(F32)
        s = jnp.sum(q8b.astype(F32) * kb, axis=-1, keepdims=True)
        if valid is not None:
            s = jnp.where(valid, s, NEG_INF)
        m_new = jnp.maximum(m, s)
        alpha = jnp.exp(m - m_new)
        p = jnp.where(s > 0.5 * NEG_INF, jnp.exp(s - m_new), 0.0)
        l = alpha * l + p
        acc = alpha * acc + p.astype(BF16).astype(F32) * vb
        return l, acc

    init = (jnp.full((nh, 1), NEG_INF, F32), jnp.zeros((nh, 1), F32), jnp.zeros((nh, LANES), F32))
    carry = lax.fori_loop(0, past // SEL_TK, tile, init)
    blk_new = past // L_SEL
    l, acc = add_new(carry, snew_ref[0, 0], snew_ref[1, 0], sel8[:, blk_new:blk_new + 1] > 0.5)
    o_s = acc * jnp.where(l > 0.0, 1.0 / l, 0.0)

    wb = win_ref.shape[3]
    wpos = past - wb + _iota((1, wb), 1)
    mw = jnp.logical_and(jnp.logical_and(wpos <= qpos, wpos > qpos - WINDOW), wpos >= 0)
    s = jnp.where(mw, _dot_nt(q8b, win_ref[0, 0, 0].astype(BF16)), NEG_INF)
    m = jnp.max(s, axis=-1, keepdims=True)
    p = jnp.where(mw, jnp.exp(s - m), 0.0)
    carry = (m, jnp.sum(p, axis=-1, keepdims=True), _dot(p.astype(BF16), win_ref[0, 0, 1].astype(BF16)))
    l, acc = add_new(carry, wnew_ref[0, 0], wnew_ref[1, 0], None)
    o_w = acc * jnp.where(l > 0.0, 1.0 / l, 0.0)

    gates = jax.nn.sigmoid(sm_ref[0])
    ys = []
    for j in range(nh):
        gl = 2 * GDN_HEADS + 3 * j
        o = gates[:, gl:gl + 1] * o_c[j:j + 1] + gates[:, gl + 1:gl + 2] * o_s[j:j + 1] + gates[:, gl + 2:gl + 3] * o_w[j:j + 1]
        ys.append(_head(o, j // NSA_REP))
    y_ref[0] = jnp.concatenate(ys, axis=-1)

    for kv in range(2):
        wout_ref[0, kv, pl.ds(0, wb - 1), :] = win_ref[0, 0, kv, pl.ds(1, wb - 1), :]
        wout_ref[0, kv, pl.ds(wb - 1, 1), :] = wnew_ref[kv, 0]


def nsa_decode(layer, page_table, q_rot, small, sel_new, win_new, cache_cmp16, cache_sel, win_state, cw):
    db, n_pages = page_table.shape
    wb = win_state.shape[3]
    past = n_pages * PAGE_SIZE
    n_sel = -(-(past + 1) // L_SEL)
    n_sel_pad = -(-n_sel // LANES) * LANES
    n_cmp = past // STRIDE_CMP
    cnt = selection_counts(n_cmp, n_sel, n_sel_pad)
    w1ab, pe16, w2 = cw
    full = lambda a: pl.BlockSpec(a.shape, lambda bb, pt: (0,) * a.ndim)
    row = lambda n: pl.BlockSpec((1, 1, n), lambda bb, pt: (bb, 0, 0))
    new = pl.BlockSpec((2, 1, 1, LANES), lambda bb, pt: (0, bb, 0, 0))
    grid_spec = pltpu.PrefetchScalarGridSpec(
        num_scalar_prefetch=1,
        grid=(db,),
        in_specs=[row(NSA_WIDTH), row(LANES), new, new,
                  pl.BlockSpec(memory_space=pl.ANY), pl.BlockSpec(memory_space=pl.ANY),
                  pl.BlockSpec((1, 1, 2, wb, LANES), lambda bb, pt: (layer, bb, 0, 0, 0)),
                  full(w1ab), full(pe16), full(w2), full(cnt)],
        out_specs=[row(NSA_WIDTH), pl.BlockSpec((1, 2, wb, LANES), lambda bb, pt: (bb, 0, 0, 0))],
        scratch_shapes=[pltpu.VMEM((2, n_cmp, STRIDE_CMP * LANES), F32), pltpu.VMEM((2, past, LANES), F32),
                        pltpu.SemaphoreType.DMA((2,))],
    )
    return pl.pallas_call(
        functools.partial(_nsa_decode_kernel, layer=layer, n_pages=n_pages, n_sel=n_sel),
        grid_spec=grid_spec,
        out_shape=[jax.ShapeDtypeStruct((db, 1, NSA_WIDTH), F32), jax.ShapeDtypeStruct((db, 2, wb, LANES), F32)],
        compiler_params=_cp("arbitrary"),
        name="nsa_decode",
    )(page_table, q_rot, small, sel_new, win_new, cache_cmp16, cache_sel, win_state, w1ab, pe16, w2, cnt)


def _col(row, eye):
    return jnp.sum(jnp.where(eye, row, 0.0), axis=1, keepdims=True)


def _gdn_step_kernel(qkv_ref, sm_ref, gate_ref, c0_ref, s0_ref, cw_ref, alog_ref, dtb_ref, ng_ref,
                     ya_ref, st_ref, cnew_ref, qs, ks, vs, gs, bs, os_):
    x = qkv_ref[...]
    y = cw_ref[0:1, :] * c0_ref[0]
    y = y + cw_ref[1:2, :] * c0_ref[1]
    y = y + cw_ref[2:3, :] * c0_ref[2]
    y = _silu(y + cw_ref[3:4, :] * x)
    cnew_ref[0] = c0_ref[1]
    cnew_ref[1] = c0_ref[2]
    cnew_ref[2] = x
    sm = sm_ref[...]
    gs[...] = jnp.exp(-jnp.exp(alog_ref[...]) * _softplus(sm + dtb_ref[...]))
    bs[...] = jax.nn.sigmoid(sm)
    qs[...] = jnp.concatenate([_l2norm(_head(y, h)) * (HEAD_DIM ** -0.5) for h in range(GDN_HEADS)], axis=-1)
    ks[...] = jnp.concatenate([_l2norm(_head(y, GDN_HEADS + h)) for h in range(GDN_HEADS)], axis=-1)
    vs[...] = y[:, 2 * GDN_WIDTH:]
    eye = _iota((HEAD_DIM, HEAD_DIM), 0) == _iota((HEAD_DIM, HEAD_DIM), 1)

    def seq(b, carry):
        row = lambda ref: ref[pl.ds(b, 1), :]
        k_r, q_r, v_r, g_r, b_r = row(ks), row(qs), row(vs), row(gs), row(bs)
        outs = []
        for h in range(GDN_HEADS):
            kcol = _col(_head(k_r, h), eye)
            qcol = _col(_head(q_r, h), eye)
            s1 = s0_ref[b, h] * g_r[:, h:h + 1]
            v_old = jnp.sum(kcol * s1, axis=0, keepdims=True)
            s2 = s1 + kcol * (b_r[:, GDN_HEADS + h:GDN_HEADS + h + 1] * (_head(v_r, h) - v_old))
            st_ref[b, h] = s2
            outs.append(jnp.sum(qcol * s2, axis=0, keepdims=True))
        os_[pl.ds(b, 1), :] = jnp.concatenate(outs, axis=-1)
        return carry

    lax.fori_loop(0, x.shape[0], seq, 0)
    o = os_[...]
    gate = gate_ref[...]
    outs = []
    for h in range(GDN_HEADS):
        oh = _head(o, h)
        ms = jnp.mean(oh * oh, axis=-1, keepdims=True)
        outs.append(oh * lax.rsqrt(ms + RMS_EPS) * ng_ref[...] * _silu(_head(gate, h)))
    ya_ref[...] = jnp.concatenate(outs, axis=-1)


def gdn_step(qkv, small, gate, conv0_t, s0, conv_w, a_log, dt_bias, norm_g):
    n = qkv.shape[0]
    vm = pl.BlockSpec(memory_space=pltpu.VMEM)
    sc = lambda w: pltpu.VMEM((n, w), F32)
    return pl.pallas_call(
        _gdn_step_kernel,
        in_specs=[vm] * 9,
        out_specs=[vm, vm, vm],
        out_shape=[jax.ShapeDtypeStruct((n, GDN_WIDTH), F32), jax.ShapeDtypeStruct(s0.shape, F32),
                   jax.ShapeDtypeStruct(conv0_t.shape, F32)],
        scratch_shapes=[sc(GDN_WIDTH), sc(GDN_WIDTH), sc(GDN_WIDTH), sc(LANES), sc(LANES), sc(GDN_WIDTH)],
        compiler_params=pltpu.CompilerParams(vmem_limit_bytes=VMEM_LIMIT),
        name="gdn_step",
    )(qkv, small, gate, conv0_t, s0, conv_w, _pad_lanes(a_log), _pad_lanes(dt_bias), norm_g.reshape(1, HEAD_DIM))


def _rwkv_step_kernel(cols_ref, sh_ref, s0_ref, mu_ref, w0_ref, w2_ref, a0_ref, a2_ref, g2_ref, kk_ref, ka_ref, rk_ref,
                      lng_ref, lnb_ref, yc_ref, st_ref, rs, ws, ks, vs, kks, as_, os_):
    x = cols_ref[...]
    xs = x + (sh_ref[...] - x) * mu_ref[...]
    r, e, k, v, kk, a, gout = _rwkv_mix(xs, w0_ref[...], w2_ref[...], a0_ref[...], a2_ref[...], g2_ref[...],
                                        kk_ref[...], ka_ref[...])
    rs[...] = r
    ws[...] = jnp.exp(-e)
    ks[...] = k
    vs[...] = v
    kks[...] = kk
    as_[...] = a
    eye = _iota((HEAD_DIM, HEAD_DIM), 0) == _iota((HEAD_DIM, HEAD_DIM), 1)

    def seq(b, carry):
        rows = [ref[pl.ds(b, 1), :] for ref in (rs, ws, ks, vs, kks, as_)]
        outs = []
        for h in range(RWKV_HEADS):
            r_r, w_r, k_r, v_r, kk_r, a_r = (_head(t, h) for t in rows)
            s0 = s0_ref[b, h]
            sa = -jnp.sum(s0 * kk_r, axis=1, keepdims=True)
            s1 = s0 * w_r + sa * (kk_r * a_r) + _col(v_r, eye) * k_r
            st_ref[b, h] = s1
            ocol = jnp.sum(s1 * r_r, axis=1, keepdims=True)
            outs.append(jnp.sum(jnp.where(eye, ocol, 0.0), axis=0, keepdims=True))
        os_[pl.ds(b, 1), :] = jnp.concatenate(outs, axis=-1)
        return carry

    lax.fori_loop(0, x.shape[0], seq, 0)
    yc_ref[...] = _rwkv_out(os_[...], r, k, v, gout, rk_ref[...], lng_ref[...], lnb_ref[...])


def rwkv_step(cols, shift0, s0, params):
    n = cols.shape[0]
    vm = pl.BlockSpec(memory_space=pltpu.VMEM)
    sc = pltpu.VMEM((n, RWKV_WIDTH), F32)
    return pl.pallas_call(
        _rwkv_step_kernel,
        in_specs=[vm] * (3 + len(params)),
        out_specs=[vm, vm],
        out_shape=[jax.ShapeDtypeStruct((n, RWKV_WIDTH), F32), jax.ShapeDtypeStruct(s0.shape, F32)],
        scratch_shapes=[sc] * 7,
        compiler_params=pltpu.CompilerParams(vmem_limit_bytes=VMEM_LIMIT),
        name="rwkv_step",
    )(cols, shift0, s0, *params)


def gdn_rwkv_params(p):
    row = lambda a: a.reshape(1, -1)
    return (row(p["mu"]), row(p["w0"]), p["w2"].astype(BF16), row(p["a0"]), p["a2"].astype(BF16),
            p["g2"].astype(BF16), row(p["k_k"]), row(p["k_a"]), row(p["r_k"]), row(p["ln_g"]), row(p["ln_b"]))


def rwkv_seq(cols, params):
    b, t, w = cols.shape
    full = lambda a: pl.BlockSpec(a.shape, lambda bb, c: (0,) * a.ndim)
    return pl.pallas_call(
        _rwkv_seq_kernel,
        grid=(b, t // CHUNK),
        in_specs=[pl.BlockSpec((1, CHUNK, w), lambda bb, c: (bb, c, 0))] + [full(a) for a in params],
        out_specs=[pl.BlockSpec((1, CHUNK, RWKV_WIDTH), lambda bb, c: (bb, c, 0)),
                   pl.BlockSpec((1, RWKV_HEADS, HEAD_DIM, HEAD_DIM), lambda bb, c: (bb, 0, 0, 0))],
        out_shape=[jax.ShapeDtypeStruct((b, t, RWKV_WIDTH), F32),
                   jax.ShapeDtypeStruct((b, RWKV_HEADS, HEAD_DIM, HEAD_DIM), F32)],
        scratch_shapes=[pltpu.VMEM((RWKV_HEADS, HEAD_DIM, HEAD_DIM), F32), pltpu.VMEM((CHUNK + 8, w), F32)],
        compiler_params=_cp("arbitrary", "arbitrary"),
        name="rwkv_seq",
    )(cols, *params)


IN_SEGS = (3 * GDN_WIDTH, GDN_WIDTH, NSA_WIDTH, 6 * NSA_KVW, RWKV_COLS, LANES)


def _regroup_w_in(w):
    gdn, nsa, rw = w[:, :GDN_COLS], w[:, GDN_COLS:GDN_COLS + NSA_COLS], w[:, GDN_COLS + NSA_COLS:]
    w3 = 3 * GDN_WIDTH
    nkv = NSA_WIDTH + 6 * NSA_KVW
    small = jnp.concatenate([gdn[:, w3:w3 + 2 * GDN_HEADS], nsa[:, nkv:]], axis=1)
    small = jnp.pad(small, ((0, 0), (0, LANES - small.shape[1])))
    return jnp.concatenate([gdn[:, :w3], gdn[:, w3 + 2 * GDN_HEADS:], nsa[:, :NSA_WIDTH], nsa[:, NSA_WIDTH:nkv],
                            rw, small], axis=1).astype(BF16)


def _layer_weights(p, l):
    rw = {k[5:]: v[l] for k, v in p.items() if k.startswith("rwkv_")}
    return dict(
        w_in=_regroup_w_in(p["w_in"][l]), w_out=p["w_out"][l].astype(BF16),
        cw=compress_weights(p["nsa_cmp_pos"][l], p["nsa_cmp_w1k"][l], p["nsa_cmp_w2k"][l], p["nsa_cmp_w1v"][l],
                            p["nsa_cmp_w2v"][l]),
        rw=gdn_rwkv_params(rw),
        w_q=p["peer_w_q"][l].astype(BF16), k1=p["peer_k1"][l].astype(BF16), k2=p["peer_k2"][l].astype(BF16),
        ut=p["peer_u"][l].T.astype(BF16), v=p["peer_v"][l].astype(BF16))


def _split_mod(ada_rows):
    return [ada_rows[:, j * D_MODEL:(j + 1) * D_MODEL] for j in range(6)]


def kernel(x_prompt, x_sample, cache_nsa_cmp, cache_nsa_sel, state_nsa_win, state_gdn, state_gdn_conv, state_rwkv, state_rwkv_shift, page_table, c_prompt, c_sample, norm_mix_g, norm_ffn_g, norm_final_g, w_ada, b_ada, w_in, w_out, gdn_conv_w, gdn_A_log, gdn_dt_bias, gdn_norm_g, nsa_cmp_pos, nsa_cmp_w1k, nsa_cmp_w2k, nsa_cmp_w1v, nsa_cmp_w2v, rwkv_mu, rwkv_w0, rwkv_w2, rwkv_a0, rwkv_a2, rwkv_g2, rwkv_k_k, rwkv_k_a, rwkv_r_k, rwkv_ln_g, rwkv_ln_b, peer_w_q, peer_k1, peer_k2, peer_u, peer_v):
    p = dict(w_in=w_in, w_out=w_out, nsa_cmp_pos=nsa_cmp_pos, nsa_cmp_w1k=nsa_cmp_w1k, nsa_cmp_w2k=nsa_cmp_w2k,
             nsa_cmp_w1v=nsa_cmp_w1v, nsa_cmp_w2v=nsa_cmp_w2v, rwkv_mu=rwkv_mu, rwkv_w0=rwkv_w0, rwkv_w2=rwkv_w2,
             rwkv_a0=rwkv_a0, rwkv_a2=rwkv_a2, rwkv_g2=rwkv_g2, rwkv_k_k=rwkv_k_k, rwkv_k_a=rwkv_k_a,
             rwkv_r_k=rwkv_r_k, rwkv_ln_g=rwkv_ln_g, rwkv_ln_b=rwkv_ln_b, peer_w_q=peer_w_q, peer_k1=peer_k1,
             peer_k2=peer_k2, peer_u=peer_u, peer_v=peer_v)
    depth = w_in.shape[0]
    bp, tp, d = x_prompt.shape
    db, ts, _ = x_sample.shape
    assert ts == 1 and tp % 512 == 0
    n_pool, n_pages = cache_nsa_cmp.shape[1], page_table.shape[1]
    past = n_pages * PAGE_SIZE
    assert (past + 1 - L_CMP) // STRIDE_CMP + 1 == past // STRIDE_CMP - 1

    n_c = bp + db
    c_all = jnp.pad(jnp.concatenate([c_prompt, c_sample], axis=0), ((0, -n_c % 8), (0, 0)))
    ada = ada_all(c_all, w_ada, b_ada)

    tab_p = rope_tables(jnp.arange(tp, dtype=jnp.int32))
    tab_s = rope_tables(jnp.full((db,), past, jnp.int32))
    cache_cmp16 = cache_nsa_cmp.reshape(depth, n_pool, 2, PAGE_SIZE // STRIDE_CMP, STRIDE_CMP * LANES)
    cache_sel = cache_nsa_sel.reshape(depth, n_pool, 2, PAGE_SIZE, LANES)
    win_state = state_nsa_win.reshape(depth, db, 2, -1, LANES)
    wb = win_state.shape[3]
    wlen = min(WINDOW, tp)

    xp = x_prompt
    xs = x_sample.reshape(1, db, d)
    outs_p = [[] for _ in range(7)]
    outs_s = [[] for _ in range(7)]
    tm_p = PEER_TM
    for l in range(depth):
        lw = _layer_weights(p, l)
        mods_p = [m[:, None, :] for m in _split_mod(ada[l, :bp])]
        mods_s = [m[None] for m in _split_mod(ada[l, bp:n_c])]

        sh1, sc1, g1, sh2, sc2, g2 = mods_p
        qkv, gate, qc, kvc, rw, small = ln_mod_matmul(xp, norm_mix_g[l], sc1, sh1, lw["w_in"], IN_SEGS, tm_p, name="in_proj")
        ya, s_gdn = gdn_seq(qkv, small, gate, gdn_conv_w[l], gdn_A_log[l], gdn_dt_bias[l], gdn_norm_g[l])
        q_rot, c_f, s_f, w_f, s_b, w_b = nsa_prep(qc, kvc, tab_p, tm_p)
        kvcmp = nsa_compress(c_f.reshape(bp, 2, tp // STRIDE_CMP, STRIDE_CMP * LANES), lw["cw"])
        yb = nsa_seq(q_rot, small, kvcmp, s_b, w_b)
        yc, s_rwkv = rwkv_seq(rw, lw["rw"])
        xp = out_proj(ya, yb, yc, lw["w_out"], xp, g1, tm_p)
        xp = peer_block(xp, norm_ffn_g[l], sc2, sh2, g2, lw["w_q"], lw["k1"], lw["k2"], lw["ut"], lw["v"], tm_p)
        kv6 = lambda a: a.reshape(bp, 2, -1, NSA_KV, HEAD_DIM)
        for o, n in zip(outs_p, (kv6(c_f), kv6(s_f), kv6(w_f[:, :, tp - wlen:]), s_gdn, qkv[:, tp - (CONV_W - 1):],
                                 s_rwkv, rw[:, tp - 1:])):
            o.append(n)

        sh1, sc1, g1, sh2, sc2, g2 = mods_s
        qkv, gate, qc, kvc, rw, small = ln_mod_matmul(xs, norm_mix_g[l], sc1, sh1, lw["w_in"], IN_SEGS, db, name="in_proj_s")
        ya, s_gdn, conv_t = gdn_step(qkv[0], small[0], gate[0], state_gdn_conv[l].transpose(1, 0, 2), state_gdn[l],
                                     gdn_conv_w[l], gdn_A_log[l], gdn_dt_bias[l], gdn_norm_g[l])
        q_rot, c_f, s_f, w_f, _, _ = nsa_prep(qc, kvc, tab_s, db)
        new_rows = lambda a: a[0][:, :, None, :]
        yb, win_new = nsa_decode(l, page_table, q_rot.reshape(db, 1, NSA_WIDTH), small.reshape(db, 1, LANES),
                                 new_rows(s_f), new_rows(w_f), cache_cmp16, cache_sel, win_state, lw["cw"])
        yc, s_rwkv = rwkv_step(rw[0], state_rwkv_shift[l][:, 0], state_rwkv[l], lw["rw"])
        xs = out_proj(ya[None], yb.reshape(1, db, NSA_WIDTH), yc[None], lw["w_out"], xs, g1, db)
        xs = peer_block(xs, norm_ffn_g[l], sc2, sh2, g2, lw["w_q"], lw["k1"], lw["k2"], lw["ut"], lw["v"], db)
        kv1 = lambda a: a[0].transpose(1, 0, 2).reshape(db, 2, 1, NSA_KV, HEAD_DIM)
        for o, n in zip(outs_s, (kv1(c_f), kv1(s_f), win_new.reshape(db, 2, wb, NSA_KV, HEAD_DIM), s_gdn,
                                 conv_t.transpose(1, 0, 2), s_rwkv, rw[0][:, None, :])):
            o.append(n)

    y_p = final_norm(xp, norm_final_g, tm_p)
    y_s = final_norm(xs, norm_final_g, db).reshape(db, 1, d)
    return (y_p, y_s, *[jnp.stack(o) for o in outs_p], *[jnp.stack(o) for o in outs_s])
```

```python
import functools
import math

import numpy as np
import jax
import jax.numpy as jnp
from jax import lax
from jax.experimental import pallas as pl
from jax.experimental.pallas import tpu as pltpu

F32 = jnp.float32
BF16 = jnp.bfloat16

D_MODEL = 1024
HEAD_DIM = 64
GDN_HEADS = 4
GDN_WIDTH = GDN_HEADS * HEAD_DIM
CONV_W = 4
CHUNK = 64
SEQ_BLK = 4 * CHUNK
NSA_HEADS = 8
NSA_KV = 2
NSA_REP = 4
NSA_WIDTH = NSA_HEADS * HEAD_DIM
NSA_KVW = NSA_KV * HEAD_DIM
L_CMP = 32
STRIDE_CMP = 16
L_SEL = 64
TOP_SEL = 16
WINDOW = 512
NSA_QBLK = 128
ROT_DIM = 16
ROPE_THETA = 500000.0
RWKV_HEADS = 4
RWKV_WIDTH = RWKV_HEADS * HEAD_DIM
LORA_W = 64
LORA_A = 64
LORA_G = 128
GDN_COLS = 4 * GDN_WIDTH + 2 * GDN_HEADS
NSA_COLS = NSA_WIDTH + 6 * NSA_KVW + 3 * NSA_HEADS
RWKV_COLS = 3 * RWKV_WIDTH + LORA_W + LORA_A + LORA_G
N_KEYS = 128
N_EXPERTS = N_KEYS * N_KEYS
PEER_HEADS = 8
PEER_TOPK = 16
D_KEY = 256
PAGE_SIZE = 128
RMS_EPS = 1e-6
GN_EPS = 64e-5
NEG_INF = -1e30
FORCE = 1e4
PAD_SCORE = -3e38

LANES = 128
VMEM_LIMIT = 56 * 1024 * 1024
PEER_TM = 256
PEER_CHUNK_I1 = 16
W_PITCH = 136
SEL_TK = 512


def _cp(*sem):
    return pltpu.CompilerParams(dimension_semantics=sem, vmem_limit_bytes=VMEM_LIMIT)


def _split(x, parts):
    out = []
    for _ in range(parts - 1):
        hi = x.astype(BF16)
        out.append(hi)
        x = x - hi.astype(F32)
    out.append(x.astype(BF16))
    return out


def _mm(a, b, dims, prec):
    if prec is None:
        return lax.dot_general(a, b, dims, preferred_element_type=F32)
    pa, pb = prec
    ap, bp = _split(a, pa), _split(b, pb)
    acc = None
    for i in range(pa):
        for j in range(pb):
            if i + j < max(pa, pb):
                t = lax.dot_general(ap[i], bp[j], dims, preferred_element_type=F32)
                acc = t if acc is None else acc + t
    return acc


HI = (2, 2)
LO = (1, 1)
LHS01 = (1, 3)
RHS_BF = (3, 1)


def _dot(a, b, prec=None):
    return _mm(a, b, (((1,), (0,)), ((), ())), prec)


def _dot_nt(a, b, prec=None):
    return _mm(a, b, (((1,), (1,)), ((), ())), prec)


def _dot_tn(a, b, prec=None):
    return _mm(a, b, (((0,), (0,)), ((), ())), prec)


def _silu(x):
    return x * jax.nn.sigmoid(x)


def _softplus(x):
    return jnp.maximum(x, 0.0) + jnp.log1p(jnp.exp(-jnp.abs(x)))


def _gelu(x):
    return 0.5 * x * (1.0 + lax.erf(x * np.float32(math.sqrt(0.5))))


def _iota(shape, dim):
    return lax.broadcasted_iota(jnp.int32, shape, dim)


def _tri_solve(L, rhs):
    x = rhs - _dot(L, rhs, HI)
    p = L
    n = L.shape[0]
    k = 2
    while k < n:
        p = _dot(p, p, HI)
        x = x + _dot(p, x, HI)
        k *= 2
    return x


def _ada_kernel(c_ref, w_ref, b_ref, o_ref):
    s = _silu(c_ref[...]).astype(BF16)
    o_ref[0] = _dot(s, w_ref[0].astype(BF16)) + b_ref[0]


def ada_all(c, w_ada, b_ada):
    depth, d, n = w_ada.shape
    r = c.shape[0]
    tn = 1536
    return pl.pallas_call(
        _ada_kernel,
        grid=(depth, n // tn),
        in_specs=[pl.BlockSpec((r, d), lambda l, j: (0, 0)),
                  pl.BlockSpec((1, d, tn), lambda l, j: (l, 0, j)),
                  pl.BlockSpec((1, 1, tn), lambda l, j: (l, 0, j))],
        out_specs=pl.BlockSpec((1, r, tn), lambda l, j: (l, 0, j)),
        out_shape=jax.ShapeDtypeStruct((depth, r, n), F32),
        compiler_params=_cp("arbitrary", "arbitrary"),
        name="ada",
    )(c, w_ada, b_ada.reshape(depth, 1, n))


def _mod_spec(mod, tm):
    if mod.shape[1] == 1:
        return pl.BlockSpec((1, 1, mod.shape[2]), lambda b, i: (b, 0, 0))
    return pl.BlockSpec((1, tm, mod.shape[2]), lambda b, i: (b, i, 0))


def _lnmm_kernel(x_ref, g_ref, sc_ref, sh_ref, w_ref, *out_refs, segs, emit_h):
    x = x_ref[0]
    ms = jnp.mean(x * x, axis=-1, keepdims=True)
    y = x * lax.rsqrt(ms + RMS_EPS) * g_ref[...]
    hb = (y * (1.0 + sc_ref[0]) + sh_ref[0]).astype(BF16)
    off = 0
    for o_ref, w in zip(out_refs, segs):
        o_ref[0] = _dot(hb, w_ref[:, off:off + w])
        off += w
    if emit_h:
        out_refs[-1][0] = hb


def ln_mod_matmul(x, gain, sc, sh, w, segs, tm, emit_h=False, name="lnmm"):
    b, t, d = x.shape
    out_shape = [jax.ShapeDtypeStruct((b, t, s), F32) for s in segs]
    out_specs = [pl.BlockSpec((1, tm, s), lambda bb, i: (bb, i, 0)) for s in segs]
    if emit_h:
        out_shape.append(jax.ShapeDtypeStruct((b, t, d), BF16))
        out_specs.append(pl.BlockSpec((1, tm, d), lambda bb, i: (bb, i, 0)))
    return pl.pallas_call(
        functools.partial(_lnmm_kernel, segs=tuple(segs), emit_h=emit_h),
        grid=(b, t // tm),
        in_specs=[pl.BlockSpec((1, tm, d), lambda bb, i: (bb, i, 0)),
                  pl.BlockSpec((1, d), lambda bb, i: (0, 0)),
                  _mod_spec(sc, tm), _mod_spec(sh, tm),
                  pl.BlockSpec(w.shape, lambda bb, i: (0, 0))],
        out_specs=out_specs,
        out_shape=out_shape,
        compiler_params=_cp("arbitrary", "arbitrary"),
        name=name,
    )(x, gain.reshape(1, d), sc, sh, w)


def _outproj_kernel(ya_ref, yb_ref, yc_ref, w_ref, x_ref, g_ref, o_ref):
    y = jnp.concatenate([ya_ref[0], yb_ref[0], yc_ref[0]], axis=-1).astype(BF16)
    o_ref[0] = x_ref[0] + g_ref[0] * _dot(y, w_ref[...])


def out_proj(ya, yb, yc, w, x, gate, tm):
    b, t, d = x.shape
    row = lambda n: pl.BlockSpec((1, tm, n), lambda bb, i: (bb, i, 0))
    return pl.pallas_call(
        _outproj_kernel,
        grid=(b, t // tm),
        in_specs=[row(ya.shape[2]), row(yb.shape[2]), row(yc.shape[2]),
                  pl.BlockSpec(w.shape, lambda bb, i: (0, 0)), row(d), _mod_spec(gate, tm)],
        out_specs=row(d),
        out_shape=jax.ShapeDtypeStruct((b, t, d), F32),
        compiler_params=_cp("arbitrary", "arbitrary"),
        name="out_proj",
    )(ya, yb, yc, w, x, gate)


def _final_norm_kernel(x_ref, g_ref, o_ref):
    x = x_ref[0]
    ms = jnp.mean(x * x, axis=-1, keepdims=True)
    o_ref[0] = x * lax.rsqrt(ms + RMS_EPS) * g_ref[...]


def final_norm(x, gain, tm):
    b, t, d = x.shape
    return pl.pallas_call(
        _final_norm_kernel,
        grid=(b, t // tm),
        in_specs=[pl.BlockSpec((1, tm, d), lambda bb, i: (bb, i, 0)), pl.BlockSpec((1, d), lambda bb, i: (0, 0))],
        out_specs=pl.BlockSpec((1, tm, d), lambda bb, i: (bb, i, 0)),
        out_shape=jax.ShapeDtypeStruct((b, t, d), F32),
        compiler_params=_cp("arbitrary", "arbitrary"),
        name="final_norm",
    )(x, gain.reshape(1, d))


def _topk_rows(x, k):
    n = x.shape[0]
    iota = _iota(x.shape, 0)
    vals, idxs = [], []
    for _ in range(k):
        m = jnp.max(x, axis=0, keepdims=True)
        idx = jnp.min(jnp.where(x == m, iota, n), axis=0, keepdims=True)
        vals.append(m)
        idxs.append(idx)
        x = jnp.where(iota == idx, -jnp.inf, x)
    return jnp.concatenate(vals, axis=0), jnp.concatenate(idxs, axis=0)


def _peer_route_kernel(q_ref, k1_ref, k2_ref, i1_ref, i2_ref, gw_ref):
    q = q_ref[0].astype(BF16)
    half = D_KEY // 2
    s1 = _dot_nt(k1_ref[0], q[:, :half])
    s2 = _dot_nt(k2_ref[0], q[:, half:])
    v1, i1 = _topk_rows(s1, PEER_TOPK)
    v2, i2 = _topk_rows(s2, PEER_TOPK)
    tm = q.shape[0]
    cand = jnp.concatenate([v1[a:a + 1] + v2 for a in range(PEER_TOPK)], axis=0)
    c1 = jnp.concatenate([jnp.broadcast_to(i1[a:a + 1], (PEER_TOPK, tm)) for a in range(PEER_TOPK)], axis=0)
    c2 = jnp.concatenate([i2] * PEER_TOPK, axis=0)
    iota = _iota(cand.shape, 0)
    n = cand.shape[0]
    sc, e1, e2 = [], [], []
    for _ in range(PEER_TOPK):
        m = jnp.max(cand, axis=0, keepdims=True)
        pos = jnp.min(jnp.where(cand == m, iota, n), axis=0, keepdims=True)
        oh = iota == pos
        sc.append(m)
        e1.append(jnp.sum(jnp.where(oh, c1, 0), axis=0, keepdims=True))
        e2.append(jnp.sum(jnp.where(oh, c2, 0), axis=0, keepdims=True))
        cand = jnp.where(oh, -jnp.inf, cand)
    sc = jnp.concatenate(sc, axis=0)
    p = jnp.exp(sc - sc[0:1])
    gw_ref[0, 0] = p / jnp.sum(p, axis=0, keepdims=True)
    i1_ref[0, 0] = jnp.concatenate(e1, axis=0).astype(F32)
    i2_ref[0, 0] = jnp.concatenate(e2, axis=0).astype(F32)


def peer_route(q, k1, k2, tm):
    b, t, _ = q.shape
    nb = t // tm
    npair = PEER_HEADS * PEER_TOPK
    out = jax.ShapeDtypeStruct((b, nb, npair, tm), F32)
    ospec = pl.BlockSpec((1, 1, PEER_TOPK, tm), lambda bb, i, h: (bb, i, h, 0))
    return pl.pallas_call(
        _peer_route_kernel,
        grid=(b, nb, PEER_HEADS),
        in_specs=[pl.BlockSpec((1, tm, D_KEY), lambda bb, i, h: (bb, i, h)),
                  pl.BlockSpec((1, N_KEYS, D_KEY // 2), lambda bb, i, h: (h, 0, 0)),
                  pl.BlockSpec((1, N_KEYS, D_KEY // 2), lambda bb, i, h: (h, 0, 0))],
        out_specs=[ospec, ospec, ospec],
        out_shape=[out, out, out],
        compiler_params=_cp("arbitrary", "arbitrary", "arbitrary"),
        name="peer_route",
    )(q, k1, k2)


def _peer_expert_kernel(h_ref, i1_ref, i2_ref, gw_ref, ut_ref, v_ref, x_ref, g_ref, o_ref,
                        wscr, acc, i1t, i2t, gwt):
    c = pl.program_id(2)
    tm = h_ref.shape[1]
    npair = i1_ref.shape[2]

    @pl.when(c == 0)
    def _():
        acc[...] = jnp.zeros_like(acc)
        i1t[...] = i1_ref[0, 0].T
        i2t[...] = i2_ref[0, 0].T
        gwt[...] = gw_ref[0, 0].T
        sub = _iota((N_KEYS, npair), 0).astype(F32)

        def tok(t, carry):
            r1 = i1t[pl.ds(t, 1), :]
            r2 = i2t[pl.ds(t, 1), :]
            gw = gwt[pl.ds(t, 1), :]
            hi = gw.astype(BF16).astype(F32)
            lo = gw - hi
            m1 = sub == r1
            lhs = jnp.concatenate([jnp.where(m1, hi, 0.0), jnp.where(m1, lo, 0.0)], axis=0).astype(BF16)
            bt = jnp.where(sub == r2, 1.0, 0.0).astype(BF16)
            w2 = _dot_nt(lhs, bt)
            wscr[pl.ds(pl.multiple_of(t * W_PITCH, 8), N_KEYS), :] = w2[:N_KEYS] + w2[N_KEYS:]
            return carry

        lax.fori_loop(0, tm, tok, 0, unroll=8)

    act = _gelu(_dot(h_ref[0], ut_ref[...]))
    parts = []
    for j in range(PEER_CHUNK_I1):
        wi = wscr[pl.ds(c * PEER_CHUNK_I1 + j, tm, stride=W_PITCH), :]
        parts.append((act[:, j * N_KEYS:(j + 1) * N_KEYS] * wi).astype(BF16))
    acc[...] += _dot(jnp.concatenate(parts, axis=1), v_ref[...])

    @pl.when(c == pl.num_programs(2) - 1)
    def _():
        o_ref[0] = x_ref[0] + g_ref[0] * acc[...]


def peer_experts(hb, i1, i2, gw, ut, v, x, gate, tm):
    b, t, d = x.shape
    ce = PEER_CHUNK_I1 * N_KEYS
    nch = N_EXPERTS // ce
    npair = PEER_HEADS * PEER_TOPK
    rspec = pl.BlockSpec((1, 1, npair, tm), lambda bb, i, c: (bb, i, 0, 0))
    gspec = (pl.BlockSpec((1, 1, d), lambda bb, i, c: (bb, 0, 0)) if gate.shape[1] == 1
             else pl.BlockSpec((1, tm, d), lambda bb, i, c: (bb, i, 0)))
    return pl.pallas_call(
        _peer_expert_kernel,
        grid=(b, t // tm, nch),
        in_specs=[pl.BlockSpec((1, tm, d), lambda bb, i, c: (bb, i, 0)), rspec, rspec, rspec,
                  pl.BlockSpec((d, ce), lambda bb, i, c: (0, c)),
                  pl.BlockSpec((ce, d), lambda bb, i, c: (c, 0)),
                  pl.BlockSpec((1, tm, d), lambda bb, i, c: (bb, i, 0)), gspec],
        out_specs=pl.BlockSpec((1, tm, d), lambda bb, i, c: (bb, i, 0)),
        out_shape=jax.ShapeDtypeStruct((b, t, d), F32),
        scratch_shapes=[pltpu.VMEM((tm * W_PITCH, N_KEYS), F32), pltpu.VMEM((tm, d), F32),
                        pltpu.VMEM((tm, npair), F32), pltpu.VMEM((tm, npair), F32), pltpu.VMEM((tm, npair), F32)],
        compiler_params=_cp("arbitrary", "arbitrary", "arbitrary"),
        name="peer_experts",
    )(hb, i1, i2, gw, ut, v, x, gate)


def peer_block(x, gain, sc, sh, gate, w_q, k1, k2, ut, v, tm):
    q, hb = ln_mod_matmul(x, gain, sc, sh, w_q, [PEER_HEADS * D_KEY], tm, emit_h=True, name="peer_query")
    i1, i2, gw = peer_route(q, k1, k2, tm)
    return peer_experts(hb, i1, i2, gw, ut, v, x, gate, tm)


def _tri_masks(n):
    r, c = _iota((n, n), 0), _iota((n, n), 1)
    return r >= c, r > c


def _l2norm(x):
    return x * lax.rsqrt(jnp.sum(x * x, axis=-1, keepdims=True) + RMS_EPS)


def _head(x, h):
    return x[:, h * HEAD_DIM:(h + 1) * HEAD_DIM]


def _gdn_seq_kernel(qkv_ref, sm_ref, gate_ref, cw_ref, alog_ref, dtb_ref, ng_ref, ya_ref, st_ref, s_scr, xbuf):
    c = pl.program_id(1)
    n = CHUNK
    nb = qkv_ref.shape[1]
    w3 = 3 * GDN_WIDTH

    @pl.when(c == 0)
    def _():
        s_scr[...] = jnp.zeros_like(s_scr)
        xbuf[0:8, :] = jnp.zeros((8, w3), F32)

    x = qkv_ref[0]
    xbuf[8:8 + nb, :] = x
    y = cw_ref[0:1, :] * xbuf[5:5 + nb, :]
    y = y + cw_ref[1:2, :] * xbuf[6:6 + nb, :]
    y = y + cw_ref[2:3, :] * xbuf[7:7 + nb, :]
    y = _silu(y + cw_ref[3:4, :] * x)
    xbuf[0:8, :] = xbuf[nb:nb + 8, :]

    sm = sm_ref[0]
    g_all = -jnp.exp(alog_ref[...]) * _softplus(sm + dtb_ref[...])
    beta_all = jax.nn.sigmoid(sm)
    incl, strict = _tri_masks(n)
    tri = jnp.where(incl, 1.0, 0.0)
    gate = gate_ref[0]

    prep = []
    for j in range(nb // n):
        rs = slice(j * n, (j + 1) * n)
        gcum = _dot(tri, g_all[rs], LHS01)
        gcum_t = gcum.T
        yj, beta = y[rs], beta_all[rs]
        heads = []
        for h in range(GDN_HEADS):
            gc = gcum[:, h:h + 1]
            gr = gcum_t[h:h + 1, :]
            decay = jnp.where(incl, jnp.exp(jnp.where(incl, gc - gr, 0.0)), 0.0)
            q = _l2norm(_head(yj, h)) * (HEAD_DIM ** -0.5)
            k = _l2norm(_head(yj, GDN_HEADS + h))
            v = _head(yj, 2 * GDN_HEADS + h)
            b = beta[:, GDN_HEADS + h:GDN_HEADS + h + 1]
            kb = k * b
            eg = jnp.exp(gc)
            kk = _dot_nt(jnp.concatenate([kb, q], axis=0), k, LO)
            lmat = jnp.where(strict, kk[:n] * decay, 0.0)
            att = kk[n:] * decay
            sol = _tri_solve(lmat, jnp.concatenate([v * b, kb * eg], axis=1))
            glast = gc[n - 1:n, :]
            heads.append((sol[:, :HEAD_DIM], jnp.concatenate([sol[:, HEAD_DIM:], q * eg], axis=0), att,
                          k * jnp.exp(glast - gc), jnp.exp(glast)))
        prep.append(heads)

    state = [s_scr[h] for h in range(GDN_HEADS)]
    rows = []
    for j, heads in enumerate(prep):
        rs = slice(j * n, (j + 1) * n)
        outs = []
        for h, (u, wq, att, kdec, eglast) in enumerate(heads):
            xs = _dot(wq, state[h], LO)
            v_new = u - xs[:n]
            o = xs[n:] + _dot(att, v_new, LO)
            state[h] = state[h] * eglast + _dot_tn(kdec, v_new, LO)
            ms = jnp.mean(o * o, axis=-1, keepdims=True)
            outs.append(o * lax.rsqrt(ms + RMS_EPS) * ng_ref[...] * _silu(_head(gate[rs], h)))
        rows.append(jnp.concatenate(outs, axis=-1))
    ya_ref[0] = jnp.concatenate(rows, axis=0)
    for h in range(GDN_HEADS):
        s_scr[h] = state[h]

    @pl.when(c == pl.num_programs(1) - 1)
    def _():
        st_ref[0] = s_scr[...]


def _pad_lanes(v, n=LANES):
    v = v.reshape(1, -1)
    return jnp.pad(v, ((0, 0), (0, n - v.shape[1])))


def gdn_seq(qkv, small, gate, conv_w, a_log, dt_bias, norm_g):
    b, t, w3 = qkv.shape
    blk = min(SEQ_BLK, t)
    full = lambda a: pl.BlockSpec(a.shape, lambda bb, c: (0,) * a.ndim)
    alog = _pad_lanes(a_log)
    dtb = _pad_lanes(dt_bias)
    ng = norm_g.reshape(1, HEAD_DIM)
    return pl.pallas_call(
        _gdn_seq_kernel,
        grid=(b, t // blk),
        in_specs=[pl.BlockSpec((1, blk, w3), lambda bb, c: (bb, c, 0)),
                  pl.BlockSpec((1, blk, LANES), lambda bb, c: (bb, c, 0)),
                  pl.BlockSpec((1, blk, GDN_WIDTH), lambda bb, c: (bb, c, 0)),
                  full(conv_w), full(alog), full(dtb), full(ng)],
        out_specs=[pl.BlockSpec((1, blk, GDN_WIDTH), lambda bb, c: (bb, c, 0)),
                   pl.BlockSpec((1, GDN_HEADS, HEAD_DIM, HEAD_DIM), lambda bb, c: (bb, 0, 0, 0))],
        out_shape=[jax.ShapeDtypeStruct((b, t, GDN_WIDTH), F32),
                   jax.ShapeDtypeStruct((b, GDN_HEADS, HEAD_DIM, HEAD_DIM), F32)],
        scratch_shapes=[pltpu.VMEM((GDN_HEADS, HEAD_DIM, HEAD_DIM), F32), pltpu.VMEM((blk + 8, w3), F32)],
        compiler_params=_cp("arbitrary", "arbitrary"),
        name="gdn_seq",
    )(qkv, small, gate, conv_w, alog, dtb, ng)


def _rwkv_mix(xs, w0, w2, a0, a2, g2, k_k, k_a):
    w = RWKV_WIDTH
    r, k, v = xs[:, :w], xs[:, w:2 * w], xs[:, 2 * w:3 * w]
    xw = xs[:, 3 * w:3 * w + LORA_W]
    xa = xs[:, 3 * w + LORA_W:3 * w + LORA_W + LORA_A]
    xg = xs[:, 3 * w + LORA_W + LORA_A:]
    wlog = -_softplus(-(w0 + _dot(jnp.tanh(xw).astype(BF16), w2))) - 0.5
    e = jnp.exp(wlog)
    a = jax.nn.sigmoid(a0 + _dot(xa.astype(BF16), a2))
    gout = _dot(jax.nn.sigmoid(xg).astype(BF16), g2)
    kkf = k * k_k
    kk = jnp.concatenate([_l2norm(_head(kkf, h)) for h in range(RWKV_HEADS)], axis=-1)
    k = k * (1.0 + (a - 1.0) * k_a)
    return r, e, k, v, kk, a, gout


def _rwkv_out(o, r, k, v, gout, r_k, ln_g, ln_b):
    outs = []
    for h in range(RWKV_HEADS):
        oh = _head(o, h)
        mean = jnp.mean(oh, axis=-1, keepdims=True)
        var = jnp.mean(jnp.square(oh - mean), axis=-1, keepdims=True)
        on = (oh - mean) * lax.rsqrt(var + GN_EPS) * _head(ln_g, h) + _head(ln_b, h)
        bonus = jnp.sum(_head(r, h) * _head(k, h) * _head(r_k, h), axis=-1, keepdims=True) * _head(v, h)
        outs.append((on + bonus) * _head(gout, h))
    return jnp.concatenate(outs, axis=-1)


def _rwkv_seq_kernel(cols_ref, mu_ref, w0_ref, w2_ref, a0_ref, a2_ref, g2_ref, kk_ref, ka_ref, rk_ref,
                     lng_ref, lnb_ref, yc_ref, st_ref, s_scr, xbuf):
    c = pl.program_id(1)
    n = CHUNK
    nb = cols_ref.shape[1]

    @pl.when(c == 0)
    def _():
        s_scr[...] = jnp.zeros_like(s_scr)
        xbuf[0:8, :] = jnp.zeros((8, RWKV_COLS), F32)

    x = cols_ref[0]
    xbuf[8:8 + nb, :] = x
    prev = xbuf[7:7 + nb, :]
    xbuf[0:8, :] = xbuf[nb:nb + 8, :]
    xs = x + (prev - x) * mu_ref[...]
    r, e, k, v, kk, a, gout = _rwkv_mix(xs, w0_ref[...], w2_ref[...], a0_ref[...], a2_ref[...], g2_ref[...],
                                        kk_ref[...], ka_ref[...])
    incl, strict = _tri_masks(n)
    tri = jnp.where(incl, 1.0, 0.0)

    prep = []
    for j in range(nb // n):
        rs = slice(j * n, (j + 1) * n)
        ej = e[rs]
        gc_all = -_dot(tri, ej, LHS01)
        heads = []
        for h in range(RWKV_HEADS):
            gc = _head(gc_all, h)
            gprev = gc + _head(ej, h)
            glast = gc[n - 1:n, :]
            rh, kh, vh, kkh = _head(r[rs], h), _head(k[rs], h), _head(v[rs], h), _head(kk[rs], h)
            bh = kkh * _head(a[rs], h)
            einv = jnp.exp(-gc)
            kap = kkh * jnp.exp(gprev)
            rt = rh * jnp.exp(gc)
            etail = jnp.exp(glast - gc)
            m = _dot_nt(jnp.concatenate([kap, rt], axis=0), jnp.concatenate([bh * einv, kh * einv], axis=0), LO)
            lb = jnp.where(strict, m[:n, :n], 0.0)
            lk = jnp.where(strict, m[:n, n:], 0.0)
            ab = jnp.where(incl, m[n:, :n], 0.0)
            ak = jnp.where(incl, m[n:, n:], 0.0)
            sol = _tri_solve(lb, jnp.concatenate([kap, _dot(lk, vh, LO)], axis=1))
            heads.append((jnp.concatenate([sol[:, :HEAD_DIM], rt], axis=0), sol[:, HEAD_DIM:], _dot(ak, vh, LO), ab,
                          _dot_tn(vh, kh * etail, LO), bh * etail, jnp.exp(glast)))
        prep.append(heads)

    state = [s_scr[h] for h in range(RWKV_HEADS)]
    rows = []
    for heads in prep:
        outs = []
        for h, (pr, p2, okv, ab, skv, btail, eglast) in enumerate(heads):
            xt = _dot_nt(pr, state[h], LO)
            u = p2 + xt[:n]
            outs.append(xt[n:] + okv - _dot(ab, u, LO))
            state[h] = state[h] * eglast + skv - _dot_tn(u, btail, LO)
        rows.append(jnp.concatenate(outs, axis=-1))
    o = jnp.concatenate(rows, axis=0)
    yc_ref[0] = _rwkv_out(o, r, k, v, gout, rk_ref[...], lng_ref[...], lnb_ref[...])
    for h in range(RWKV_HEADS):
        s_scr[h] = state[h]

    @pl.when(c == pl.num_programs(1) - 1)
    def _():
        st_ref[0] = s_scr[...]


def rope_tables(pos):
    half = ROT_DIM // 2
    inv = ROPE_THETA ** (-jnp.arange(half, dtype=F32) / half)
    ang = pos.astype(F32)[:, None] * inv[None, :]
    cos, sin = jnp.cos(ang), jnp.sin(ang)
    n = pos.shape[0]
    one = jnp.ones((n, HEAD_DIM - ROT_DIM), F32)
    zero = jnp.zeros((n, HEAD_DIM - ROT_DIM), F32)
    z8 = jnp.zeros((n, half), F32)
    tc = jnp.concatenate([cos, cos, one], axis=1)
    t1 = jnp.concatenate([-sin, z8, zero], axis=1)
    t2 = jnp.concatenate([z8, sin, zero], axis=1)
    rep = LANES // HEAD_DIM
    return jnp.tile(tc, (1, rep)), jnp.tile(t1, (1, rep)), jnp.tile(t2, (1, rep))


def _rope(x, tc, t1, t2):
    half = ROT_DIM // 2
    return x * tc + pltpu.roll(x, LANES - half, 1) * t1 + pltpu.roll(x, half, 1) * t2


def _nsa_prep_kernel(q_ref, kv_ref, tc_ref, t1_ref, t2_ref, qo_ref, cmp_ref, sel_ref, win_ref, selb_ref, winb_ref):
    tc, t1, t2 = tc_ref[...], t1_ref[...], t2_ref[...]
    q = q_ref[0]
    scale = HEAD_DIM ** -0.5
    qo_ref[0] = jnp.concatenate(
        [_rope(q[:, j * LANES:(j + 1) * LANES], tc, t1, t2) * scale for j in range(NSA_WIDTH // LANES)],
        axis=-1).astype(BF16)
    kv = kv_ref[0]
    for br, (o_ref, ob_ref) in enumerate(((cmp_ref, None), (sel_ref, selb_ref), (win_ref, winb_ref))):
        k = _rope(kv[:, (2 * br) * LANES:(2 * br + 1) * LANES], tc, t1, t2)
        v = kv[:, (2 * br + 1) * LANES:(2 * br + 2) * LANES]
        o_ref[0, 0] = k
        o_ref[0, 1] = v
        if ob_ref is not None:
            ob_ref[0, 0] = k.astype(BF16)
            ob_ref[0, 1] = v.astype(BF16)


def nsa_prep(qc, kvc, tables, tm):
    b, t, _ = qc.shape
    kvs = jax.ShapeDtypeStruct((b, 2, t, LANES), F32)
    kvb = jax.ShapeDtypeStruct((b, 2, t, LANES), BF16)
    kspec = pl.BlockSpec((1, 2, tm, LANES), lambda bb, i: (bb, 0, i, 0))
    tspec = pl.BlockSpec((tm, LANES), lambda bb, i: (i, 0))
    return pl.pallas_call(
        _nsa_prep_kernel,
        grid=(b, t // tm),
        in_specs=[pl.BlockSpec((1, tm, NSA_WIDTH), lambda bb, i: (bb, i, 0)),
                  pl.BlockSpec((1, tm, 6 * LANES), lambda bb, i: (bb, i, 0)), tspec, tspec, tspec],
        out_specs=[pl.BlockSpec((1, tm, NSA_WIDTH), lambda bb, i: (bb, i, 0)), kspec, kspec, kspec, kspec, kspec],
        out_shape=[jax.ShapeDtypeStruct((b, t, NSA_WIDTH), BF16), kvs, kvs, kvs, kvb, kvb],
        compiler_params=_cp("arbitrary", "arbitrary"),
        name="nsa_prep",
    )(qc, kvc, *tables)


def _compress_rows(r16, w1ab, pe16, w2):
    pq = _dot(r16.astype(BF16), w1ab)
    n16 = pq.shape[0]
    nxt = pltpu.roll(pq[:, LANES:], n16 - 1, 0)
    pqe = _dot(pe16, w1ab, RHS_BF)
    bias = pqe[0:1, :LANES] + pqe[1:2, LANES:]
    return _dot(_gelu(pq[:, :LANES] + nxt + bias).astype(BF16), w2)


def _nsa_compress_kernel(x_ref, w1_ref, pe_ref, w2_ref, o_ref):
    o_ref[0, 0] = _compress_rows(x_ref[0, 0], w1_ref[0], pe_ref[...], w2_ref[0])


def compress_weights(pe, w1k, w2k, w1v, w2v):
    eye = jnp.eye(NSA_KV, dtype=F32)

    def one(w1, w2):
        w1r = w1.reshape(L_CMP, HEAD_DIM, HEAD_DIM)
        bd = (w1r[:, None, :, None, :] * eye[None, :, None, :, None]).reshape(L_CMP * LANES, LANES)
        half = STRIDE_CMP * LANES
        w1ab = jnp.concatenate([bd[:half], bd[half:]], axis=1)
        w2bd = (w2[None, :, None, :] * eye[:, None, :, None]).reshape(LANES, LANES)
        return w1ab.astype(BF16), w2bd.astype(BF16)

    k, v = one(w1k, w2k), one(w1v, w2v)
    pe16 = jnp.tile(pe, (1, NSA_KV)).reshape(L_CMP // STRIDE_CMP, STRIDE_CMP * LANES)
    pe16 = jnp.pad(pe16, ((0, 8 - pe16.shape[0]), (0, 0)))
    return jnp.stack([k[0], v[0]]), pe16, jnp.stack([k[1], v[1]])


def nsa_compress(kv16, cw):
    b, _, n16, w = kv16.shape
    w1ab, pe16, w2 = cw
    return pl.pallas_call(
        _nsa_compress_kernel,
        grid=(b, 2),
        in_specs=[pl.BlockSpec((1, 1, n16, w), lambda bb, j: (bb, j, 0, 0)),
                  pl.BlockSpec((1,) + w1ab.shape[1:], lambda bb, j: (j, 0, 0)),
                  pl.BlockSpec(pe16.shape, lambda bb, j: (0, 0)),
                  pl.BlockSpec((1, LANES, LANES), lambda bb, j: (j, 0, 0))],
        out_specs=pl.BlockSpec((1, 1, n16, LANES), lambda bb, j: (bb, j, 0, 0)),
        out_shape=jax.ShapeDtypeStruct((b, 2, n16, LANES), F32),
        compiler_params=_cp("arbitrary", "arbitrary"),
        name="nsa_compress",
    )(kv16, w1ab, pe16, w2)


def selection_counts(n_cmp_pad, n_sel, n_sel_pad):
    i = np.arange(n_cmp_pad)[:, None, None, None]
    j = np.arange(n_sel)[None, :, None, None]
    m = np.arange(L_SEL // STRIDE_CMP)[None, None, :, None]
    n = np.arange(L_CMP // STRIDE_CMP)[None, None, None, :]
    cnt = np.sum(i == (L_SEL // STRIDE_CMP) * j + m - n, axis=(2, 3)).astype(np.float32)
    return jnp.asarray(np.pad(cnt, ((0, 0), (0, n_sel_pad - n_sel))))


def _topk_mask_lanes(x, k):
    iota = _iota(x.shape, 1)
    big = x.shape[1]
    sel = jnp.zeros(x.shape, F32)
    for _ in range(k):
        m = jnp.max(x, axis=-1, keepdims=True)
        idx = jnp.min(jnp.where(x == m, iota, big), axis=-1, keepdims=True)
        oh = iota == idx
        sel = jnp.where(oh, 1.0, sel)
        x = jnp.where(oh, -jnp.inf, x)
    return sel


def _topk_mask_rows(x, k):
    iota = _iota(x.shape, 0)
    big = x.shape[0]
    sel = jnp.zeros(x.shape, F32)
    for _ in range(k):
        m = jnp.max(x, axis=0, keepdims=True)
        idx = jnp.min(jnp.where(x == m, iota, big), axis=0, keepdims=True)
        oh = iota == idx
        sel = jnp.where(oh, 1.0, sel)
        x = jnp.where(oh, -jnp.inf, x)
    return sel


def _masked_softmax(s, mask):
    sm = jnp.where(mask, s, NEG_INF)
    p = jnp.exp(sm - jnp.max(sm, axis=-1, keepdims=True))
    p = p / jnp.sum(p, axis=-1, keepdims=True)
    return jnp.where(mask, p, 0.0)


def _select_blocks(pcs, cnt, qpos_col, n_sel, transposed=False):
    imp = _dot(pcs, cnt, RHS_BF)
    blk = _iota(imp.shape, 1)
    cur = qpos_col // L_SEL
    imp = jnp.where(blk == 0, FORCE, jnp.where(blk == cur, FORCE, jnp.where(blk == cur - 1, FORCE, imp)))
    imp = jnp.where(blk > cur, -FORCE, imp)
    imp = jnp.where(blk >= n_sel, PAD_SCORE, imp)
    if transposed:
        return _topk_mask_rows(imp.T, min(TOP_SEL, n_sel)).T
    return _topk_mask_lanes(imp, min(TOP_SEL, n_sel))


def _expand_sel(sel, k0, tk):
    nsp = sel.shape[1]
    e = _iota((nsp, tk), 0) == (k0 + _iota((nsp, tk), 1)) // L_SEL
    return _dot(sel.astype(BF16), jnp.where(e, 1.0, 0.0).astype(BF16))


def _nsa_seq_kernel(q_ref, sm_ref, kc_ref, ks_ref, kw_ref, cnt_ref, y_ref, *, t_len, n_sel):
    i = pl.program_id(1)
    nq = NSA_QBLK
    rows = NSA_REP * nq
    s0 = i * nq
    qpos = s0 + _iota((rows, 1), 0) % nq
    qpos_q = s0 + _iota((nq, 1), 0)
    gates = jax.nn.sigmoid(sm_ref[0])
    n_cmp = kc_ref.shape[2]
    cend = _iota((1, n_cmp), 1) * STRIDE_CMP + (L_CMP - 1)
    wlen = WINDOW + nq
    ws = pl.multiple_of(jnp.maximum(s0 - WINDOW, 0), nq)
    wpos = ws + _iota((1, wlen), 1)
    n_tiles = (s0 + nq + SEL_TK - 1) // SEL_TK
    ys = []
    for g in range(NSA_KV):
        lo, hi = g * HEAD_DIM, (g + 1) * HEAD_DIM
        qg = q_ref[0][:, g * NSA_REP * HEAD_DIM:(g + 1) * NSA_REP * HEAD_DIM]
        qq = jnp.concatenate([_head(qg, r) for r in range(NSA_REP)], axis=0)
        kc = kc_ref[0, 0][:, lo:hi].astype(BF16)
        vc = kc_ref[0, 1][:, lo:hi].astype(BF16)
        pc = _masked_softmax(_dot_nt(qq, kc), cend <= qpos)
        o_c = _dot(pc.astype(BF16), vc)
        pcs = pc[0:nq]
        for r in range(1, NSA_REP):
            pcs = pcs + pc[r * nq:(r + 1) * nq]
        sel = _select_blocks(pcs, cnt_ref[...], qpos_q, n_sel, transposed=True)

        def tile(kt, carry):
            m, l, acc = carry
            k0 = pl.multiple_of(kt * SEL_TK, SEL_TK)
            kt_ = ks_ref[0, 0, pl.ds(k0, SEL_TK), :][:, lo:hi]
            vt_ = ks_ref[0, 1, pl.ds(k0, SEL_TK), :][:, lo:hi]
            s = _dot_nt(qq, kt_)
            me = _expand_sel(sel, k0, SEL_TK)
            me = jnp.concatenate([me] * NSA_REP, axis=0)
            kpos = k0 + _iota((1, SEL_TK), 1)
            s = jnp.where(kpos <= qpos, jnp.where(me > 0.5, s, NEG_INF), NEG_INF)
            m_new = jnp.maximum(m, jnp.max(s, axis=-1, keepdims=True))
            alpha = jnp.exp(m - m_new)
            p = jnp.where(s > 0.5 * NEG_INF, jnp.exp(s - m_new), 0.0)
            l = alpha * l + jnp.sum(p, axis=-1, keepdims=True)
            acc = alpha * acc + _dot(p.astype(BF16), vt_)
            return m_new, l, acc

        init = (jnp.full((rows, 1), NEG_INF, F32), jnp.zeros((rows, 1), F32), jnp.zeros((rows, HEAD_DIM), F32))
        _, l, acc = lax.fori_loop(0, n_tiles, tile, init)
        o_s = acc * jnp.where(l > 0.0, 1.0 / l, 0.0)
        kw = kw_ref[0, 0, pl.ds(ws, wlen), :][:, lo:hi]
        vw = kw_ref[0, 1, pl.ds(ws, wlen), :][:, lo:hi]
        mw = jnp.logical_and(wpos <= qpos, wpos > qpos - WINDOW)
        pw = _masked_softmax(_dot_nt(qq, kw), mw)
        o_w = _dot(pw.astype(BF16), vw)
        for r in range(NSA_REP):
            gl = 2 * GDN_HEADS + (g * NSA_REP + r) * 3
            gc, gs, gw = (gates[:, gl + j:gl + j + 1] for j in range(3))
            sl = slice(r * nq, (r + 1) * nq)
            ys.append(gc * o_c[sl] + gs * o_s[sl] + gw * o_w[sl])
    y_ref[0] = jnp.concatenate(ys, axis=-1)


def nsa_seq(q_rot, small, kvc_cmp, sel_b, win_b):
    b, t, _ = q_rot.shape
    assert t % SEL_TK == 0 and t >= WINDOW + NSA_QBLK
    n_sel = t // L_SEL
    n_sel_pad = -(-n_sel // LANES) * LANES
    n_cmp = kvc_cmp.shape[2]
    cnt = selection_counts(n_cmp, n_sel, n_sel_pad)
    whole = lambda a: pl.BlockSpec((1,) + a.shape[1:], lambda bb, i: (bb,) + (0,) * (a.ndim - 1))
    return pl.pallas_call(
        functools.partial(_nsa_seq_kernel, t_len=t, n_sel=n_sel),
        grid=(b, t // NSA_QBLK),
        in_specs=[pl.BlockSpec((1, NSA_QBLK, NSA_WIDTH), lambda bb, i: (bb, i, 0)),
                  pl.BlockSpec((1, NSA_QBLK, LANES), lambda bb, i: (bb, i, 0)),
                  whole(kvc_cmp), whole(sel_b), whole(win_b),
                  pl.BlockSpec(cnt.shape, lambda bb, i: (0, 0))],
        out_specs=pl.BlockSpec((1, NSA_QBLK, NSA_WIDTH), lambda bb, i: (bb, i, 0)),
        out_shape=jax.ShapeDtypeStruct((b, t, NSA_WIDTH), F32),
        compiler_params=_cp("arbitrary", "arbitrary"),
        name="nsa_seq",
    )(q_rot, small, kvc_cmp, sel_b, win_b, cnt)


def _nsa_decode_kernel(pt_ref, q_ref, sm_ref, snew_ref, wnew_ref, cc_ref, cs_ref, win_ref, w1_ref, pe_ref, w2_ref,
                       cnt_ref, y_ref, wout_ref, cbuf, sbuf, sem, *, layer, n_pages, n_sel):
    b = pl.program_id(0)
    past = n_pages * PAGE_SIZE
    rows16 = PAGE_SIZE // STRIDE_CMP

    def cmp_copy(page, p):
        return pltpu.make_async_copy(cc_ref.at[layer, page], cbuf.at[:, pl.ds(p * rows16, rows16), :], sem.at[0])

    def sel_copy(page, p):
        return pltpu.make_async_copy(cs_ref.at[layer, page], sbuf.at[:, pl.ds(p * PAGE_SIZE, PAGE_SIZE), :], sem.at[1])

    def issue(p, carry):
        page = pt_ref[b, p]
        cmp_copy(page, p).start()
        sel_copy(page, p).start()
        return carry

    def wait(p, carry):
        page = pt_ref[b, p]
        cmp_copy(page, p).wait()
        sel_copy(page, p).wait()
        return carry

    lax.fori_loop(0, n_pages, issue, 0)
    lax.fori_loop(0, n_pages, wait, 0)

    nh = NSA_HEADS
    qf = q_ref[0].astype(F32)
    zero = jnp.zeros((1, HEAD_DIM), F32)
    q8 = jnp.concatenate(
        [jnp.concatenate([_head(qf, j), zero] if j < NSA_REP else [zero, _head(qf, j)], axis=1) for j in range(nh)],
        axis=0)
    q8b = q8.astype(BF16)
    qpos = jnp.full((nh, 1), past, jnp.int32)

    kc = _compress_rows(cbuf[0], w1_ref[0], pe_ref[...], w2_ref[0])
    vc = _compress_rows(cbuf[1], w1_ref[1], pe_ref[...], w2_ref[1])
    n_cmp = kc.shape[0]
    cend = _iota((1, n_cmp), 1) * STRIDE_CMP + (L_CMP - 1)
    pc = _masked_softmax(_dot_nt(q8b, kc.astype(BF16)), cend <= qpos)
    o_c = _dot(pc.astype(BF16), vc.astype(BF16))
    pcs = jnp.concatenate([jnp.sum(pc[g * NSA_REP:(g + 1) * NSA_REP], axis=0, keepdims=True) for g in range(NSA_KV)]
                          + [jnp.zeros((nh - NSA_KV, n_cmp), F32)], axis=0)
    sel = _select_blocks(pcs, cnt_ref[...], qpos, n_sel)
    sel8 = jnp.concatenate([jnp.broadcast_to(sel[g:g + 1], (NSA_REP, sel.shape[1])) for g in range(NSA_KV)], axis=0)

    def tile(kt, carry):
        m, l, acc = carry
        k0 = pl.multiple_of(kt * SEL_TK, SEL_TK)
        s = _dot_nt(q8b, sbuf[0, pl.ds(k0, SEL_TK), :].astype(BF16))
        s = jnp.where(_expand_sel(sel8, k0, SEL_TK) > 0.5, s, NEG_INF)
        m_new = jnp.maximum(m, jnp.max(s, axis=-1, keepdims=True))
        alpha = jnp.exp(m - m_new)
        p = jnp.where(s > 0.5 * NEG_INF, jnp.exp(s - m_new), 0.0)
        l = alpha * l + jnp.sum(p, axis=-1, keepdims=True)
        acc = alpha * acc + _dot(p.astype(BF16), sbuf[1, pl.ds(k0, SEL_TK), :].astype(BF16))
        return m_new, l, acc

    def add_new(carry, k_row, v_row, valid):
        m, l, acc = carry
        kb = k_row.astype(BF16).astype(F32)
        vb = v_row.astype(BF16).astype(F32)
        s = jnp.sum(q8b.astype(F32) * kb, axis=-1, keepdims=True)
        if valid is not None:
            s = jnp.where(valid, s, NEG_INF)
        m_new = jnp.maximum(m, s)
        alpha = jnp.exp(m - m_new)
        p = jnp.where(s > 0.5 * NEG_INF, jnp.exp(s - m_new), 0.0)
        l = alpha * l + p
        acc = alpha * acc + p.astype(BF16).astype(F32) * vb
        return l, acc

    init = (jnp.full((nh, 1), NEG_INF, F32), jnp.zeros((nh, 1), F32), jnp.zeros((nh, LANES), F32))
    carry = lax.fori_loop(0, past // SEL_TK, tile, init)
    blk_new = past // L_SEL
    l, acc = add_new(carry, snew_ref[0, 0], snew_ref[1, 0], sel8[:, blk_new:blk_new + 1] > 0.5)
    o_s = acc * jnp.where(l > 0.0, 1.0 / l, 0.0)

    wb = win_ref.shape[3]
    wpos = past - wb + _iota((1, wb), 1)
    mw = jnp.logical_and(jnp.logical_and(wpos <= qpos, wpos > qpos - WINDOW), wpos >= 0)
    s = jnp.where(mw, _dot_nt(q8b, win_ref[0, 0, 0].astype(BF16)), NEG_INF)
    m = jnp.max(s, axis=-1, keepdims=True)
    p = jnp.where(mw, jnp.exp(s - m), 0.0)
    carry = (m, jnp.sum(p, axis=-1, keepdims=True), _dot(p.astype(BF16), win_ref[0, 0, 1].astype(BF16)))
    l, acc = add_new(carry, wnew_ref[0, 0], wnew_ref[1, 0], None)
    o_w = acc * jnp.where(l > 0.0, 1.0 / l, 0.0)

    gates = jax.nn.sigmoid(sm_ref[0])
    ys = []
    for j in range(nh):
        gl = 2 * GDN_HEADS + 3 * j
        o = gates[:, gl:gl + 1] * o_c[j:j + 1] + gates[:, gl + 1:gl + 2] * o_s[j:j + 1] + gates[:, gl + 2:gl + 3] * o_w[j:j + 1]
        ys.append(_head(o, j // NSA_REP))
    y_ref[0] = jnp.concatenate(ys, axis=-1)

    for kv in range(2):
        wout_ref[0, kv, pl.ds(0, wb - 1), :] = win_ref[0, 0, kv, pl.ds(1, wb - 1), :]
        wout_ref[0, kv, pl.ds(wb - 1, 1), :] = wnew_ref[kv, 0]


def nsa_decode(layer, page_table, q_rot, small, sel_new, win_new, cache_cmp16, cache_sel, win_state, cw):
    db, n_pages = page_table.shape
    wb = win_state.shape[3]
    past = n_pages * PAGE_SIZE
    n_sel = -(-(past + 1) // L_SEL)
    n_sel_pad = -(-n_sel // LANES) * LANES
    n_cmp = past // STRIDE_CMP
    cnt = selection_counts(n_cmp, n_sel, n_sel_pad)
    w1ab, pe16, w2 = cw
    full = lambda a: pl.BlockSpec(a.shape, lambda bb, pt: (0,) * a.ndim)
    row = lambda n: pl.BlockSpec((1, 1, n), lambda bb, pt: (bb, 0, 0))
    new = pl.BlockSpec((2, 1, 1, LANES), lambda bb, pt: (0, bb, 0, 0))
    grid_spec = pltpu.PrefetchScalarGridSpec(
        num_scalar_prefetch=1,
        grid=(db,),
        in_specs=[row(NSA_WIDTH), row(LANES), new, new,
                  pl.BlockSpec(memory_space=pl.ANY), pl.BlockSpec(memory_space=pl.ANY),
                  pl.BlockSpec((1, 1, 2, wb, LANES), lambda bb, pt: (layer, bb, 0, 0, 0)),
                  full(w1ab), full(pe16), full(w2), full(cnt)],
        out_specs=[row(NSA_WIDTH), pl.BlockSpec((1, 2, wb, LANES), lambda bb, pt: (bb, 0, 0, 0))],
        scratch_shapes=[pltpu.VMEM((2, n_cmp, STRIDE_CMP * LANES), F32), pltpu.VMEM((2, past, LANES), F32),
                        pltpu.SemaphoreType.DMA((2,))],
    )
    return pl.pallas_call(
        functools.partial(_nsa_decode_kernel, layer=layer, n_pages=n_pages, n_sel=n_sel),
        grid_spec=grid_spec,
        out_shape=[jax.ShapeDtypeStruct((db, 1, NSA_WIDTH), F32), jax.ShapeDtypeStruct((db, 2, wb, LANES), F32)],
        compiler_params=_cp("arbitrary"),
        name="nsa_decode",
    )(page_table, q_rot, small, sel_new, win_new, cache_cmp16, cache_sel, win_state, w1ab, pe16, w2, cnt)


def _col(row, eye):
    return jnp.sum(jnp.where(eye, row, 0.0), axis=1, keepdims=True)


def _gdn_step_kernel(qkv_ref, sm_ref, gate_ref, c0_ref, s0_ref, cw_ref, alog_ref, dtb_ref, ng_ref,
                     ya_ref, st_ref, cnew_ref, qs, ks, vs, gs, bs, os_):
    x = qkv_ref[...]
    y = cw_ref[0:1, :] * c0_ref[0]
    y = y + cw_ref[1:2, :] * c0_ref[1]
    y = y + cw_ref[2:3, :] * c0_ref[2]
    y = _silu(y + cw_ref[3:4, :] * x)
    cnew_ref[0] = c0_ref[1]
    cnew_ref[1] = c0_ref[2]
    cnew_ref[2] = x
    sm = sm_ref[...]
    gs[...] = jnp.exp(-jnp.exp(alog_ref[...]) * _softplus(sm + dtb_ref[...]))
    bs[...] = jax.nn.sigmoid(sm)
    qs[...] = jnp.concatenate([_l2norm(_head(y, h)) * (HEAD_DIM ** -0.5) for h in range(GDN_HEADS)], axis=-1)
    ks[...] = jnp.concatenate([_l2norm(_head(y, GDN_HEADS + h)) for h in range(GDN_HEADS)], axis=-1)
    vs[...] = y[:, 2 * GDN_WIDTH:]
    eye = _iota((HEAD_DIM, HEAD_DIM), 0) == _iota((HEAD_DIM, HEAD_DIM), 1)

    def seq(b, carry):
        row = lambda ref: ref[pl.ds(b, 1), :]
        k_r, q_r, v_r, g_r, b_r = row(ks), row(qs), row(vs), row(gs), row(bs)
        outs = []
        for h in range(GDN_HEADS):
            kcol = _col(_head(k_r, h), eye)
            qcol = _col(_head(q_r, h), eye)
            s1 = s0_ref[b, h] * g_r[:, h:h + 1]
            v_old = jnp.sum(kcol * s1, axis=0, keepdims=True)
            s2 = s1 + kcol * (b_r[:, GDN_HEADS + h:GDN_HEADS + h + 1] * (_head(v_r, h) - v_old))
            st_ref[b, h] = s2
            outs.append(jnp.sum(qcol * s2, axis=0, keepdims=True))
        os_[pl.ds(b, 1), :] = jnp.concatenate(outs, axis=-1)
        return carry

    lax.fori_loop(0, x.shape[0], seq, 0)
    o = os_[...]
    gate = gate_ref[...]
    outs = []
    for h in range(GDN_HEADS):
        oh = _head(o, h)
        ms = jnp.mean(oh * oh, axis=-1, keepdims=True)
        outs.append(oh * lax.rsqrt(ms + RMS_EPS) * ng_ref[...] * _silu(_head(gate, h)))
    ya_ref[...] = jnp.concatenate(outs, axis=-1)


def gdn_step(qkv, small, gate, conv0_t, s0, conv_w, a_log, dt_bias, norm_g):
    n = qkv.shape[0]
    vm = pl.BlockSpec(memory_space=pltpu.VMEM)
    sc = lambda w: pltpu.VMEM((n, w), F32)
    return pl.pallas_call(
        _gdn_step_kernel,
        in_specs=[vm] * 9,
        out_specs=[vm, vm, vm],
        out_shape=[jax.ShapeDtypeStruct((n, GDN_WIDTH), F32), jax.ShapeDtypeStruct(s0.shape, F32),
                   jax.ShapeDtypeStruct(conv0_t.shape, F32)],
        scratch_shapes=[sc(GDN_WIDTH), sc(GDN_WIDTH), sc(GDN_WIDTH), sc(LANES), sc(LANES), sc(GDN_WIDTH)],
        compiler_params=pltpu.CompilerParams(vmem_limit_bytes=VMEM_LIMIT),
        name="gdn_step",
    )(qkv, small, gate, conv0_t, s0, conv_w, _pad_lanes(a_log), _pad_lanes(dt_bias), norm_g.reshape(1, HEAD_DIM))


def _rwkv_step_kernel(cols_ref, sh_ref, s0_ref, mu_ref, w0_ref, w2_ref, a0_ref, a2_ref, g2_ref, kk_ref, ka_ref, rk_ref,
                      lng_ref, lnb_ref, yc_ref, st_ref, rs, ws, ks, vs, kks, as_, os_):
    x = cols_ref[...]
    xs = x + (sh_ref[...] - x) * mu_ref[...]
    r, e, k, v, kk, a, gout = _rwkv_mix(xs, w0_ref[...], w2_ref[...], a0_ref[...], a2_ref[...], g2_ref[...],
                                        kk_ref[...], ka_ref[...])
    rs[...] = r
    ws[...] = jnp.exp(-e)
    ks[...] = k
    vs[...] = v
    kks[...] = kk
    as_[...] = a
    eye = _iota((HEAD_DIM, HEAD_DIM), 0) == _iota((HEAD_DIM, HEAD_DIM), 1)

    def seq(b, carry):
        rows = [ref[pl.ds(b, 1), :] for ref in (rs, ws, ks, vs, kks, as_)]
        outs = []
        for h in range(RWKV_HEADS):
            r_r, w_r, k_r, v_r, kk_r, a_r = (_head(t, h) for t in rows)
            s0 = s0_ref[b, h]
            sa = -jnp.sum(s0 * kk_r, axis=1, keepdims=True)
            s1 = s0 * w_r + sa * (kk_r * a_r) + _col(v_r, eye) * k_r
            st_ref[b, h] = s1
            ocol = jnp.sum(s1 * r_r, axis=1, keepdims=True)
            outs.append(jnp.sum(jnp.where(eye, ocol, 0.0), axis=0, keepdims=True))
        os_[pl.ds(b, 1), :] = jnp.concatenate(outs, axis=-1)
        return carry

    lax.fori_loop(0, x.shape[0], seq, 0)
    yc_ref[...] = _rwkv_out(os_[...], r, k, v, gout, rk_ref[...], lng_ref[...], lnb_ref[...])


def rwkv_step(cols, shift0, s0, params):
    n = cols.shape[0]
    vm = pl.BlockSpec(memory_space=pltpu.VMEM)
    sc = pltpu.VMEM((n, RWKV_WIDTH), F32)
    return pl.pallas_call(
        _rwkv_step_kernel,
        in_specs=[vm] * (3 + len(params)),
        out_specs=[vm, vm],
        out_shape=[jax.ShapeDtypeStruct((n, RWKV_WIDTH), F32), jax.ShapeDtypeStruct(s0.shape, F32)],
        scratch_shapes=[sc] * 7,
        compiler_params=pltpu.CompilerParams(vmem_limit_bytes=VMEM_LIMIT),
        name="rwkv_step",
    )(cols, shift0, s0, *params)


def gdn_rwkv_params(p):
    row = lambda a: a.reshape(1, -1)
    return (row(p["mu"]), row(p["w0"]), p["w2"].astype(BF16), row(p["a0"]), p["a2"].astype(BF16),
            p["g2"].astype(BF16), row(p["k_k"]), row(p["k_a"]), row(p["r_k"]), row(p["ln_g"]), row(p["ln_b"]))


def rwkv_seq(cols, params):
    b, t, w = cols.shape
    blk = min(SEQ_BLK, t)
    full = lambda a: pl.BlockSpec(a.shape, lambda bb, c: (0,) * a.ndim)
    return pl.pallas_call(
        _rwkv_seq_kernel,
        grid=(b, t // blk),
        in_specs=[pl.BlockSpec((1, blk, w), lambda bb, c: (bb, c, 0))] + [full(a) for a in params],
        out_specs=[pl.BlockSpec((1, blk, RWKV_WIDTH), lambda bb, c: (bb, c, 0)),
                   pl.BlockSpec((1, RWKV_HEADS, HEAD_DIM, HEAD_DIM), lambda bb, c: (bb, 0, 0, 0))],
        out_shape=[jax.ShapeDtypeStruct((b, t, RWKV_WIDTH), F32),
                   jax.ShapeDtypeStruct((b, RWKV_HEADS, HEAD_DIM, HEAD_DIM), F32)],
        scratch_shapes=[pltpu.VMEM((RWKV_HEADS, HEAD_DIM, HEAD_DIM), F32), pltpu.VMEM((blk + 8, w), F32)],
        compiler_params=_cp("arbitrary", "arbitrary"),
        name="rwkv_seq",
    )(cols, *params)


IN_SEGS = (3 * GDN_WIDTH, GDN_WIDTH, NSA_WIDTH, 6 * NSA_KVW, RWKV_COLS, LANES)


def _regroup_w_in(w):
    gdn, nsa, rw = w[:, :GDN_COLS], w[:, GDN_COLS:GDN_COLS + NSA_COLS], w[:, GDN_COLS + NSA_COLS:]
    w3 = 3 * GDN_WIDTH
    nkv = NSA_WIDTH + 6 * NSA_KVW
    small = jnp.concatenate([gdn[:, w3:w3 + 2 * GDN_HEADS], nsa[:, nkv:]], axis=1)
    small = jnp.pad(small, ((0, 0), (0, LANES - small.shape[1])))
    return jnp.concatenate([gdn[:, :w3], gdn[:, w3 + 2 * GDN_HEADS:], nsa[:, :NSA_WIDTH], nsa[:, NSA_WIDTH:nkv],
                            rw, small], axis=1).astype(BF16)


def _layer_weights(p, l):
    rw = {k[5:]: v[l] for k, v in p.items() if k.startswith("rwkv_")}
    return dict(
        w_in=_regroup_w_in(p["w_in"][l]), w_out=p["w_out"][l].astype(BF16),
        cw=compress_weights(p["nsa_cmp_pos"][l], p["nsa_cmp_w1k"][l], p["nsa_cmp_w2k"][l], p["nsa_cmp_w1v"][l],
                            p["nsa_cmp_w2v"][l]),
        rw=gdn_rwkv_params(rw),
        w_q=p["peer_w_q"][l].astype(BF16), k1=p["peer_k1"][l].astype(BF16), k2=p["peer_k2"][l].astype(BF16),
        ut=p["peer_u"][l].T.astype(BF16), v=p["peer_v"][l].astype(BF16))


def _split_mod(ada_rows):
    return [ada_rows[:, j * D_MODEL:(j + 1) * D_MODEL] for j in range(6)]


def kernel(x_prompt, x_sample, cache_nsa_cmp, cache_nsa_sel, state_nsa_win, state_gdn, state_gdn_conv, state_rwkv, state_rwkv_shift, page_table, c_prompt, c_sample, norm_mix_g, norm_ffn_g, norm_final_g, w_ada, b_ada, w_in, w_out, gdn_conv_w, gdn_A_log, gdn_dt_bias, gdn_norm_g, nsa_cmp_pos, nsa_cmp_w1k, nsa_cmp_w2k, nsa_cmp_w1v, nsa_cmp_w2v, rwkv_mu, rwkv_w0, rwkv_w2, rwkv_a0, rwkv_a2, rwkv_g2, rwkv_k_k, rwkv_k_a, rwkv_r_k, rwkv_ln_g, rwkv_ln_b, peer_w_q, peer_k1, peer_k2, peer_u, peer_v):
    p = dict(w_in=w_in, w_out=w_out, nsa_cmp_pos=nsa_cmp_pos, nsa_cmp_w1k=nsa_cmp_w1k, nsa_cmp_w2k=nsa_cmp_w2k,
             nsa_cmp_w1v=nsa_cmp_w1v, nsa_cmp_w2v=nsa_cmp_w2v, rwkv_mu=rwkv_mu, rwkv_w0=rwkv_w0, rwkv_w2=rwkv_w2,
             rwkv_a0=rwkv_a0, rwkv_a2=rwkv_a2, rwkv_g2=rwkv_g2, rwkv_k_k=rwkv_k_k, rwkv_k_a=rwkv_k_a,
             rwkv_r_k=rwkv_r_k, rwkv_ln_g=rwkv_ln_g, rwkv_ln_b=rwkv_ln_b, peer_w_q=peer_w_q, peer_k1=peer_k1,
             peer_k2=peer_k2, peer_u=peer_u, peer_v=peer_v)
    depth = w_in.shape[0]
    bp, tp, d = x_prompt.shape
    db, ts, _ = x_sample.shape
    assert ts == 1 and tp % 512 == 0
    n_pool, n_pages = cache_nsa_cmp.shape[1], page_table.shape[1]
    past = n_pages * PAGE_SIZE
    assert (past + 1 - L_CMP) // STRIDE_CMP + 1 == past // STRIDE_CMP - 1

    n_c = bp + db
    c_all = jnp.pad(jnp.concatenate([c_prompt, c_sample], axis=0), ((0, -n_c % 8), (0, 0)))
    ada = ada_all(c_all, w_ada, b_ada)

    tab_p = rope_tables(jnp.arange(tp, dtype=jnp.int32))
    tab_s = rope_tables(jnp.full((db,), past, jnp.int32))
    cache_cmp16 = cache_nsa_cmp.reshape(depth, n_pool, 2, PAGE_SIZE // STRIDE_CMP, STRIDE_CMP * LANES)
    cache_sel = cache_nsa_sel.reshape(depth, n_pool, 2, PAGE_SIZE, LANES)
    win_state = state_nsa_win.reshape(depth, db, 2, -1, LANES)
    wb = win_state.shape[3]
    wlen = min(WINDOW, tp)

    xp = x_prompt
    xs = x_sample.reshape(1, db, d)
    outs_p = [[] for _ in range(7)]
    outs_s = [[] for _ in range(7)]
    tm_p = PEER_TM
    for l in range(depth):
        lw = _layer_weights(p, l)
        mods_p = [m[:, None, :] for m in _split_mod(ada[l, :bp])]
        mods_s = [m[None] for m in _split_mod(ada[l, bp:n_c])]

        sh1, sc1, g1, sh2, sc2, g2 = mods_p
        qkv, gate, qc, kvc, rw, small = ln_mod_matmul(xp, norm_mix_g[l], sc1, sh1, lw["w_in"], IN_SEGS, tm_p, name="in_proj")
        ya, s_gdn = gdn_seq(qkv, small, gate, gdn_conv_w[l], gdn_A_log[l], gdn_dt_bias[l], gdn_norm_g[l])
        q_rot, c_f, s_f, w_f, s_b, w_b = nsa_prep(qc, kvc, tab_p, tm_p)
        kvcmp = nsa_compress(c_f.reshape(bp, 2, tp // STRIDE_CMP, STRIDE_CMP * LANES), lw["cw"])
        yb = nsa_seq(q_rot, small, kvcmp, s_b, w_b)
        yc, s_rwkv = rwkv_seq(rw, lw["rw"])
        xp = out_proj(ya, yb, yc, lw["w_out"], xp, g1, tm_p)
        xp = peer_block(xp, norm_ffn_g[l], sc2, sh2, g2, lw["w_q"], lw["k1"], lw["k2"], lw["ut"], lw["v"], tm_p)
        kv6 = lambda a: a.reshape(bp, 2, -1, NSA_KV, HEAD_DIM)
        for o, n in zip(outs_p, (kv6(c_f), kv6(s_f), kv6(w_f[:, :, tp - wlen:]), s_gdn, qkv[:, tp - (CONV_W - 1):],
                                 s_rwkv, rw[:, tp - 1:])):
            o.append(n)

        sh1, sc1, g1, sh2, sc2, g2 = mods_s
        qkv, gate, qc, kvc, rw, small = ln_mod_matmul(xs, norm_mix_g[l], sc1, sh1, lw["w_in"], IN_SEGS, db, name="in_proj_s")
        ya, s_gdn, conv_t = gdn_step(qkv[0], small[0], gate[0], state_gdn_conv[l].transpose(1, 0, 2), state_gdn[l],
                                     gdn_conv_w[l], gdn_A_log[l], gdn_dt_bias[l], gdn_norm_g[l])
        q_rot, c_f, s_f, w_f, _, _ = nsa_prep(qc, kvc, tab_s, db)
        new_rows = lambda a: a[0][:, :, None, :]
        yb, win_new = nsa_decode(l, page_table, q_rot.reshape(db, 1, NSA_WIDTH), small.reshape(db, 1, LANES),
                                 new_rows(s_f), new_rows(w_f), cache_cmp16, cache_sel, win_state, lw["cw"])
        yc, s_rwkv = rwkv_step(rw[0], state_rwkv_shift[l][:, 0], state_rwkv[l], lw["rw"])
        xs = out_proj(ya[None], yb.reshape(1, db, NSA_WIDTH), yc[None], lw["w_out"], xs, g1, db)
        xs = peer_block(xs, norm_ffn_g[l], sc2, sh2, g2, lw["w_q"], lw["k1"], lw["k2"], lw["ut"], lw["v"], db)
        kv1 = lambda a: a[0].transpose(1, 0, 2).reshape(db, 2, 1, NSA_KV, HEAD_DIM)
        for o, n in zip(outs_s, (kv1(c_f), kv1(s_f), win_new.reshape(db, 2, wb, NSA_KV, HEAD_DIM), s_gdn,
                                 conv_t.transpose(1, 0, 2), s_rwkv, rw[0][:, None, :])):
            o.append(n)

    y_p = final_norm(xp, norm_final_g, tm_p)
    y_s = final_norm(xs, norm_final_g, db).reshape(db, 1, d)
    return (y_p, y_s, *[jnp.stack(o) for o in outs_p], *[jnp.stack(o) for o in outs_s])
```
